```python
import math
import jax, jax.numpy as jnp
from jax import lax
import numpy as np

D_MODEL = 4096
BATCH = 4
SEQ = 4096
DEPTH = 2

MEM_LEN = 256
MIX_W = D_MODEL
ATTN_HEAD_DIM = 128
ATTN_W = MIX_W // 2
N_ATTN_HEADS = ATTN_W // ATTN_HEAD_DIM
CONV_W = MIX_W // 4
SSM_W = MIX_W - ATTN_W - CONV_W
CONV_WIDTH = 31
SSM_GROUP_CH = 16
SSM_GROUPS = SSM_W // SSM_GROUP_CH
SSM_STATE = 64
DILATION_PATTERNS = ((128, 1), (512, 4), (2048, 16))
N_MEM_HEADS = 4
MEM_HEAD_DIM = 128
MEM_W = N_MEM_HEADS * MEM_HEAD_DIM
D_FF = 4 * D_MODEL
IN_W = 3 * ATTN_W + 2 * CONV_W + SSM_W
EPS = 1e-6
NEG_INF = -1e30

kernel_name = "hybrid_dilattn_conformer_s5_block"


def rms_norm(x, g):
    xf = x.astype(jnp.float32)
    y = xf * lax.rsqrt(jnp.mean(xf * xf, axis=-1, keepdims=True) + EPS)
    return (y * g.astype(jnp.float32)).astype(x.dtype)


def layer_norm(x, g, b):
    xf = x.astype(jnp.float32)
    mu = jnp.mean(xf, axis=-1, keepdims=True)
    xc = xf - mu
    y = xc * lax.rsqrt(jnp.mean(xc * xc, axis=-1, keepdims=True) + EPS)
    return (y * g.astype(jnp.float32) + b.astype(jnp.float32)).astype(x.dtype)


def dilated_window_attention(q, k, v, window, dilation):
    bsz, seq, heads, hd = q.shape
    blk = window // dilation
    sub_len = seq // dilation
    n_blk = -(-sub_len // blk)
    pad_len = n_blk * blk - sub_len

    def to_blocks(t):
        t = t.reshape(bsz, sub_len, dilation, heads, hd).transpose(0, 2, 3, 1, 4)
        t = jnp.pad(t, ((0, 0), (0, 0), (0, 0), (0, pad_len), (0, 0)))
        return t.reshape(bsz, dilation, heads, n_blk, blk, hd)

    def with_prev(t):
        prev = jnp.pad(t, ((0, 0), (0, 0), (0, 0), (1, 0), (0, 0), (0, 0)))[:, :, :, :-1]
        return jnp.concatenate([prev, t], axis=-2)

    qb = to_blocks(q)
    kb = with_prev(to_blocks(k))
    vb = with_prev(to_blocks(v))
    s = jnp.einsum('brhnqe,brhnke->brhnqk', qb, kb)
    qi = jnp.arange(blk)[:, None]
    kj = jnp.arange(2 * blk)[None, :]
    dist = qi + blk - kj
    bi = jnp.arange(n_blk)[:, None, None]
    valid = (dist >= 0) & (dist <= blk) & (bi * blk + kj - blk >= 0)
    s = jnp.where(valid, s, NEG_INF)
    m = jnp.max(s, axis=-1, keepdims=True)
    p = jnp.exp(s - m)
    den = jnp.sum(p, axis=-1, keepdims=True)
    o = jnp.einsum('brhnqk,brhnke->brhnqe', p, vb) / den
    lse = (m + jnp.log(den))[..., 0]
    o = o.reshape(bsz, dilation, heads, n_blk * blk, hd)[:, :, :, :sub_len]
    o = o.transpose(0, 3, 1, 2, 4).reshape(bsz, seq, heads, hd)
    lse = lse.reshape(bsz, dilation, heads, n_blk * blk)[..., :sub_len]
    lse = lse.transpose(0, 3, 1, 2).reshape(bsz, seq, heads)
    return o, lse


def dilated_mixture_attention(q, k, v, q_g, k_g):
    bsz, seq, _ = q.shape
    shp = (bsz, seq, N_ATTN_HEADS, ATTN_HEAD_DIM)
    qf = rms_norm(q.reshape(shp).astype(jnp.float32), q_g) * (ATTN_HEAD_DIM ** -0.5)
    kf = rms_norm(k.reshape(shp).astype(jnp.float32), k_g)
    vf = v.reshape(shp).astype(jnp.float32)
    outs, lses = [], []
    for window, dilation in DILATION_PATTERNS:
        o, lse = dilated_window_attention(qf, kf, vf, window, dilation)
        outs.append(o)
        lses.append(lse)
    wts = jax.nn.softmax(jnp.stack(lses, axis=0), axis=0)
    o = jnp.sum(wts[..., None] * jnp.stack(outs, axis=0), axis=0)
    return o.reshape(bsz, seq, ATTN_W).astype(q.dtype)


def conformer_conv(a, gate, dw, dw_b, ln_g, ln_b):
    h = a * jax.nn.sigmoid(gate)
    h = lax.conv_general_dilated(h, dw.astype(h.dtype), window_strides=(1,),
                                 padding=[(CONV_WIDTH - 1, 0)],
                                 dimension_numbers=('NWC', 'WIO', 'NWC'),
                                 feature_group_count=CONV_W)
    h = h + dw_b.astype(h.dtype)
    return jax.nn.silu(layer_norm(h, ln_g, ln_b))


def s5_glu(u, a_re, a_im, b_re, b_im, c_re, c_im, d_skip, log_step, w_glu, b_glu):
    bsz, seq, _ = u.shape
    f32 = jnp.float32
    uf = u.astype(f32).reshape(bsz, seq, SSM_GROUPS, SSM_GROUP_CH)
    lam = lax.complex(a_re.astype(f32), a_im.astype(f32))
    step = jnp.exp(log_step.astype(f32))[:, None]
    lam_bar = jnp.exp(lam * step)
    b = lax.complex(b_re.astype(f32), b_im.astype(f32))
    b_bar = ((lam_bar - 1.0) / lam)[..., None] * b
    bu = jnp.einsum('bsgh,gph->bsgp', uf.astype(jnp.complex64), b_bar)
    a_seq = jnp.broadcast_to(lam_bar, bu.shape)

    def combine(left, right):
        a_l, x_l = left
        a_r, x_r = right
        return a_r * a_l, a_r * x_l + x_r

    _, states = lax.associative_scan(combine, (a_seq, bu), axis=1)
    c = lax.complex(c_re.astype(f32), c_im.astype(f32))
    y = jnp.einsum('bsgp,ghp->bsgh', states, c).real
    y = y + d_skip.astype(f32).reshape(SSM_GROUPS, SSM_GROUP_CH) * uf
    z = jax.nn.gelu(y.reshape(bsz, seq, SSM_W))
    out = z * jax.nn.sigmoid(z @ w_glu.astype(f32) + b_glu.astype(f32))
    return out.astype(u.dtype)


def memory_cross_attention(h, mem_n, w_cq, w_ckv, cq_g, ck_g, w_co):
    bsz, seq, _ = h.shape
    q = (h @ w_cq).reshape(bsz, seq, N_MEM_HEADS, MEM_HEAD_DIM).astype(jnp.float32)
    kv = mem_n @ w_ckv
    k, v = jnp.split(kv, 2, axis=-1)
    mlen = mem_n.shape[1]
    k = k.reshape(bsz, mlen, N_MEM_HEADS, MEM_HEAD_DIM).astype(jnp.float32)
    v = v.reshape(bsz, mlen, N_MEM_HEADS, MEM_HEAD_DIM).astype(jnp.float32)
    q = rms_norm(q, cq_g) * (MEM_HEAD_DIM ** -0.5)
    k = rms_norm(k, ck_g)
    p = jax.nn.softmax(jnp.einsum('bshe,bmhe->bhsm', q, k), axis=-1)
    o = jnp.einsum('bhsm,bmhe->bshe', p, v).reshape(bsz, seq, MEM_W).astype(h.dtype)
    return o @ w_co


def setup_inputs(seed: int = 0) -> dict:
    key = jax.random.key(seed)
    ks = jax.random.split(key, 40)
    f32 = jnp.float32

    def nrm(k, shape, scale):
        return jax.random.normal(k, shape, f32) * scale

    def gain(k, shape):
        return 1.0 + 0.01 * jax.random.normal(k, shape, f32)

    L = DEPTH
    a_im = jnp.pi * jnp.broadcast_to(jnp.arange(SSM_STATE, dtype=f32), (L, SSM_GROUPS, SSM_STATE))
    return {
        "x": jax.random.normal(ks[0], (BATCH, SEQ, D_MODEL), f32),
        "mem": jax.random.normal(ks[1], (BATCH, MEM_LEN, D_MODEL), f32),
        "norm_mix": gain(ks[2], (L, D_MODEL)),
        "w_in": nrm(ks[3], (L, D_MODEL, IN_W), D_MODEL ** -0.5),
        "q_norm": gain(ks[4], (L, ATTN_HEAD_DIM)),
        "k_norm": gain(ks[5], (L, ATTN_HEAD_DIM)),
        "conv_dw": nrm(ks[6], (L, CONV_WIDTH, 1, CONV_W), CONV_WIDTH ** -0.5),
        "conv_b": nrm(ks[7], (L, CONV_W), 0.01),
        "conv_ln_g": gain(ks[8], (L, CONV_W)),
        "conv_ln_b": nrm(ks[9], (L, CONV_W), 0.01),
        "ssm_a_re": -0.5 + nrm(ks[10], (L, SSM_GROUPS, SSM_STATE), 0.01),
        "ssm_a_im": a_im + nrm(ks[11], (L, SSM_GROUPS, SSM_STATE), 0.01),
        "ssm_b_re": nrm(ks[12], (L, SSM_GROUPS, SSM_STATE, SSM_GROUP_CH), (2 * SSM_GROUP_CH) ** -0.5),
        "ssm_b_im": nrm(ks[13], (L, SSM_GROUPS, SSM_STATE, SSM_GROUP_CH), (2 * SSM_GROUP_CH) ** -0.5),
        "ssm_c_re": nrm(ks[14], (L, SSM_GROUPS, SSM_GROUP_CH, SSM_STATE), (2 * SSM_STATE) ** -0.5),
        "ssm_c_im": nrm(ks[15], (L, SSM_GROUPS, SSM_GROUP_CH, SSM_STATE), (2 * SSM_STATE) ** -0.5),
        "ssm_d": nrm(ks[16], (L, SSM_W), 1.0),
        "ssm_log_step": jax.random.uniform(ks[17], (L, SSM_GROUPS), f32,
                                           minval=math.log(1e-3), maxval=math.log(1e-1)),
        "ssm_w_glu": nrm(ks[18], (L, SSM_W, SSM_W), SSM_W ** -0.5),
        "ssm_b_glu": nrm(ks[19], (L, SSM_W), 0.01),
        "mix_out_norm": gain(ks[20], (L, MIX_W)),
        "w_out": nrm(ks[21], (L, MIX_W, D_MODEL), MIX_W ** -0.5),
        "norm_cross": gain(ks[22], (L, D_MODEL)),
        "norm_mem": gain(ks[23], (L, D_MODEL)),
        "w_cq": nrm(ks[24], (L, D_MODEL, MEM_W), D_MODEL ** -0.5),
        "w_ckv": nrm(ks[25], (L, D_MODEL, 2 * MEM_W), D_MODEL ** -0.5),
        "cq_norm": gain(ks[26], (L, MEM_HEAD_DIM)),
        "ck_norm": gain(ks[27], (L, MEM_HEAD_DIM)),
        "w_co": nrm(ks[28], (L, MEM_W, D_MODEL), MEM_W ** -0.5),
        "norm_mlp": gain(ks[29], (L, D_MODEL)),
        "w_up": nrm(ks[30], (L, D_MODEL, D_FF), D_MODEL ** -0.5),
        "w_down": nrm(ks[31], (L, D_FF, D_MODEL), D_FF ** -0.5),
    }


def reference(x, mem, norm_mix, w_in, q_norm, k_norm, conv_dw, conv_b, conv_ln_g, conv_ln_b,
              ssm_a_re, ssm_a_im, ssm_b_re, ssm_b_im, ssm_c_re, ssm_c_im, ssm_d, ssm_log_step,
              ssm_w_glu, ssm_b_glu, mix_out_norm, w_out, norm_cross, norm_mem, w_cq, w_ckv,
              cq_norm, ck_norm, w_co, norm_mlp, w_up, w_down):
    split_at = [ATTN_W, 2 * ATTN_W, 3 * ATTN_W, 3 * ATTN_W + CONV_W, 3 * ATTN_W + 2 * CONV_W]
    for l in range(DEPTH):
        h = rms_norm(x, norm_mix[l])
        proj = h @ w_in[l]
        q, k, v, conv_a, conv_g, ssm_u = jnp.split(proj, split_at, axis=-1)
        attn = dilated_mixture_attention(q, k, v, q_norm[l], k_norm[l])
        conv = conformer_conv(conv_a, conv_g, conv_dw[l], conv_b[l], conv_ln_g[l], conv_ln_b[l])
        ssm = s5_glu(ssm_u, ssm_a_re[l], ssm_a_im[l], ssm_b_re[l], ssm_b_im[l], ssm_c_re[l],
                     ssm_c_im[l], ssm_d[l], ssm_log_step[l], ssm_w_glu[l], ssm_b_glu[l])
        g = mix_out_norm[l]
        mixed = jnp.concatenate([
            rms_norm(attn, g[:ATTN_W]),
            rms_norm(conv, g[ATTN_W:ATTN_W + CONV_W]),
            rms_norm(ssm, g[ATTN_W + CONV_W:]),
        ], axis=-1)
        x = x + mixed @ w_out[l]
        x = x + memory_cross_attention(rms_norm(x, norm_cross[l]), rms_norm(mem, norm_mem[l]),
                                       w_cq[l], w_ckv[l], cq_norm[l], ck_norm[l], w_co[l])
        h = rms_norm(x, norm_mlp[l])
        x = x + jnp.square(jax.nn.relu(h @ w_up[l])) @ w_down[l]
    return x
```

```python
import functools
import math

import jax
import jax.numpy as jnp
from jax import lax
from jax.experimental import pallas as pl
from jax.experimental.pallas import tpu as pltpu

F32 = jnp.float32
BF16 = jnp.bfloat16

EPS = 1e-6
NEG_INF = -1e30
DILATION_PATTERNS = ((128, 1), (512, 4), (2048, 16))
N_MEM_HEADS = 4

V7X_LANES = 128
V7X_VMEM_LIMIT_BYTES = 60000 * 1024
SSM_CHUNK = 16


def _tile(n, pref):
    if n <= pref:
        return n
    assert n % pref == 0, (n, pref)
    return pref


def _params(*semantics):
    return pltpu.CompilerParams(dimension_semantics=semantics, vmem_limit_bytes=V7X_VMEM_LIMIT_BYTES)


def _rms_rows(x, g):
    return x * lax.rsqrt(jnp.mean(x * x, axis=-1, keepdims=True) + EPS) * g


def _norm_matmul_body(x_ref, g_ref, w_ref, o_ref, h_ref, *, act):
    @pl.when(pl.program_id(1) == 0)
    def _():
        h_ref[...] = _rms_rows(x_ref[...], g_ref[...]).astype(h_ref.dtype)

    acc = jnp.dot(h_ref[...], w_ref[...], preferred_element_type=F32)
    if act == "relu2":
        acc = jnp.square(jnp.maximum(acc, 0.0))
    o_ref[...] = acc.astype(o_ref.dtype)


def _norm_matmul(x, g, w, *, act=None, out_dtype=F32, tm=512, tn=1024):
    m, k = x.shape
    n = w.shape[1]
    tm, tn = _tile(m, tm), _tile(n, tn)
    return pl.pallas_call(
        functools.partial(_norm_matmul_body, act=act),
        grid=(m // tm, n // tn),
        in_specs=[
            pl.BlockSpec((tm, k), lambda i, j: (i, 0)),
            pl.BlockSpec((1, k), lambda i, j: (0, 0)),
            pl.BlockSpec((k, tn), lambda i, j: (0, j)),
        ],
        out_specs=pl.BlockSpec((tm, tn), lambda i, j: (i, j)),
        out_shape=jax.ShapeDtypeStruct((m, n), out_dtype),
        scratch_shapes=[pltpu.VMEM((tm, k), BF16)],
        compiler_params=_params("parallel", "arbitrary"),
        name="norm_matmul",
    )(x, g.reshape(1, k), w)


def _matmul_res_body(a_ref, w_ref, r_ref, o_ref, acc_ref):
    kk = pl.program_id(2)

    @pl.when(kk == 0)
    def _():
        acc_ref[...] = jnp.zeros_like(acc_ref)

    acc_ref[...] += jnp.dot(a_ref[...], w_ref[...], preferred_element_type=F32)

    @pl.when(kk == pl.num_programs(2) - 1)
    def _():
        o_ref[...] = r_ref[...] + acc_ref[...]


def _matmul_res(a, w, res, *, tm=1024, tn=1024, tk=2048):
    m, k = a.shape
    n = w.shape[1]
    tm, tn, tk = _tile(m, tm), _tile(n, tn), _tile(k, tk)
    return pl.pallas_call(
        _matmul_res_body,
        grid=(m // tm, n // tn, k // tk),
        in_specs=[
            pl.BlockSpec((tm, tk), lambda i, j, kk: (i, kk)),
            pl.BlockSpec((tk, tn), lambda i, j, kk: (kk, j)),
            pl.BlockSpec((tm, tn), lambda i, j, kk: (i, j)),
        ],
        out_specs=pl.BlockSpec((tm, tn), lambda i, j, kk: (i, j)),
        out_shape=jax.ShapeDtypeStruct((m, n), F32),
        scratch_shapes=[pltpu.VMEM((tm, tn), F32)],
        compiler_params=_params("parallel", "parallel", "arbitrary"),
        name="matmul_res",
    )(a, w, res)


def _out_proj_body(a_ref, c_ref, s_ref, w_ref, r_ref, o_ref):
    ka, kc = a_ref.shape[1], c_ref.shape[1]
    acc = jnp.dot(a_ref[...], w_ref[0:ka, :], preferred_element_type=F32)
    acc += jnp.dot(c_ref[...], w_ref[ka:ka + kc, :], preferred_element_type=F32)
    acc += jnp.dot(s_ref[...], w_ref[ka + kc:, :], preferred_element_type=F32)
    o_ref[...] = r_ref[...] + acc


def _out_proj(attn, conv, ssm, w, res, *, tm=512, tn=1024):
    m, n = res.shape
    ka, kc, ks = attn.shape[1], conv.shape[1], ssm.shape[1]
    tm, tn = _tile(m, tm), _tile(n, tn)
    return pl.pallas_call(
        _out_proj_body,
        grid=(m // tm, n // tn),
        in_specs=[
            pl.BlockSpec((tm, ka), lambda i, j: (i, 0)),
            pl.BlockSpec((tm, kc), lambda i, j: (i, 0)),
            pl.BlockSpec((tm, ks), lambda i, j: (i, 0)),
            pl.BlockSpec((ka + kc + ks, tn), lambda i, j: (0, j)),
            pl.BlockSpec((tm, tn), lambda i, j: (i, j)),
        ],
        out_specs=pl.BlockSpec((tm, tn), lambda i, j: (i, j)),
        out_shape=jax.ShapeDtypeStruct((m, n), F32),
        compiler_params=_params("parallel", "parallel"),
        name="out_proj",
    )(attn, conv, ssm, w, res)


def _attn_body(q_ref, km_ref, kh_ref, vm_ref, vh_ref, qg_ref, kg_ref, o_ref, lse_ref, kcat_ref, vcat_ref,
               *, n_heads, hd, blk, scale):
    tq = q_ref.shape[0]
    nq = tq // blk
    i = pl.program_id(2)
    qg, kg = qg_ref[...], kg_ref[...]

    for h in range(n_heads):
        cs = slice(h * hd, (h + 1) * hd)
        kcat_ref[0:blk, cs] = _rms_rows(kh_ref[:, cs], kg).astype(BF16)
        kcat_ref[blk:blk + tq, cs] = _rms_rows(km_ref[:, cs], kg).astype(BF16)
    vcat_ref[0:blk, :] = vh_ref[...].astype(BF16)
    vcat_ref[blk:blk + tq, :] = vm_ref[...].astype(BF16)

    qi = lax.broadcasted_iota(jnp.int32, (blk, 2 * blk), 0)
    kj = lax.broadcasted_iota(jnp.int32, (blk, 2 * blk), 1)
    band = (kj >= qi) & (kj <= qi + blk)
    lanes_per_head = V7X_LANES // n_heads
    lane_head = lax.broadcasted_iota(jnp.int32, (blk, V7X_LANES), 1) // lanes_per_head

    for n in range(nq):
        valid = band & (kj >= jnp.where(i * nq + n > 0, 0, blk)) if n == 0 else band
        lse_tile = jnp.zeros((blk, V7X_LANES), F32)
        for h in range(n_heads):
            cs = slice(h * hd, (h + 1) * hd)
            qn = (_rms_rows(q_ref[n * blk:(n + 1) * blk, cs], qg) * scale).astype(BF16)
            k = kcat_ref[n * blk:(n + 2) * blk, cs]
            s = lax.dot_general(qn, k, (((1,), (1,)), ((), ())), preferred_element_type=F32)
            s = jnp.where(valid, s, NEG_INF)
            m = jnp.max(s, axis=-1, keepdims=True)
            p = jnp.exp(s - m)
            den = jnp.sum(p, axis=-1, keepdims=True)
            v = vcat_ref[n * blk:(n + 2) * blk, cs]
            o_ref[n * blk:(n + 1) * blk, cs] = jnp.dot(p.astype(BF16), v, preferred_element_type=F32) / den
            lse_tile = jnp.where(lane_head == h, m + jnp.log(den), lse_tile)
        lse_ref[n * blk:(n + 1) * blk, :] = lse_tile


def _dilated_attention(proj, q_g, k_g, *, bsz, seq, n_heads, hd, window, dilation, tq=256):
    n_cols = proj.shape[1]
    aw = n_heads * hd
    assert window % dilation == 0 and seq % dilation == 0 and n_cols % aw == 0
    assert V7X_LANES % n_heads == 0
    blk = window // dilation
    sub = seq // dilation
    assert sub % blk == 0
    tq = _tile(sub, tq)
    assert tq % blk == 0
    cpr = n_cols // aw
    view = proj.reshape(bsz, sub, dilation * n_cols)
    hpt = tq // blk

    def main(which):
        return pl.BlockSpec((None, tq, aw), lambda b, r, i: (b, i, r * cpr + which))

    def halo(which):
        return pl.BlockSpec((None, blk, aw), lambda b, r, i: (b, jnp.maximum(i * hpt - 1, 0), r * cpr + which))

    gain = pl.BlockSpec((1, hd), lambda b, r, i: (0, 0))
    o, lse = pl.pallas_call(
        functools.partial(_attn_body, n_heads=n_heads, hd=hd, blk=blk, scale=hd ** -0.5),
        grid=(bsz, dilation, sub // tq),
        in_specs=[main(0), main(1), halo(1), main(2), halo(2), gain, gain],
        out_specs=[
            pl.BlockSpec((None, tq, aw), lambda b, r, i: (b, i, r)),
            pl.BlockSpec((None, tq, V7X_LANES), lambda b, r, i: (b, i, r)),
        ],
        out_shape=[
            jax.ShapeDtypeStruct((bsz, sub, dilation * aw), F32),
            jax.ShapeDtypeStruct((bsz, sub, dilation * V7X_LANES), F32),
        ],
        scratch_shapes=[pltpu.VMEM((blk + tq, aw), BF16), pltpu.VMEM((blk + tq, aw), BF16)],
        compiler_params=_params("parallel", "parallel", "parallel"),
        name=f"dilated_attn_d{dilation}",
    )(view, view, view, view, view, q_g.reshape(1, hd), k_g.reshape(1, hd))
    return o.reshape(bsz * seq, aw), lse.reshape(bsz * seq, V7X_LANES)


def _attn_merge_body(*refs, n_pat, n_heads, hd):
    o_refs, l_refs = refs[:n_pat], refs[n_pat:2 * n_pat]
    g_ref, out_ref, acc_ref = refs[2 * n_pat:]
    lses = [r[...] for r in l_refs]
    m = functools.reduce(jnp.maximum, lses)
    es = [jnp.exp(l - m) for l in lses]
    tot = functools.reduce(lambda a, b: a + b, es)
    ws = [e / tot for e in es]
    lanes_per_head = V7X_LANES // n_heads
    ssq = jnp.zeros((out_ref.shape[0], 1), F32)
    for h in range(n_heads):
        cs = slice(h * hd, (h + 1) * hd)
        lc = slice(h * lanes_per_head, h * lanes_per_head + 1)
        acc = ws[0][:, lc] * o_refs[0][:, cs]
        for p in range(1, n_pat):
            acc += ws[p][:, lc] * o_refs[p][:, cs]
        acc_ref[:, cs] = acc
        ssq += jnp.sum(acc * acc, axis=-1, keepdims=True)
    inv = lax.rsqrt(ssq / (n_heads * hd) + EPS)
    out_ref[...] = (acc_ref[...] * inv * g_ref[...]).astype(out_ref.dtype)


def _attn_merge(outs, lses, g, *, n_heads, hd, tm=256):
    m, aw = outs[0].shape
    tm = _tile(m, tm)
    n_pat = len(outs)
    row = lambda w: pl.BlockSpec((tm, w), lambda i: (i, 0))
    return pl.pallas_call(
        functools.partial(_attn_merge_body, n_pat=n_pat, n_heads=n_heads, hd=hd),
        grid=(m // tm,),
        in_specs=[row(aw)] * n_pat + [row(V7X_LANES)] * n_pat + [pl.BlockSpec((1, aw), lambda i: (0, 0))],
        out_specs=row(aw),
        out_shape=jax.ShapeDtypeStruct((m, aw), BF16),
        scratch_shapes=[pltpu.VMEM((tm, aw), F32)],
        compiler_params=_params("parallel"),
        name="attn_merge",
    )(*outs, *lses, g.reshape(1, aw))


CONV_HALO_ROWS = 32
CONV_ROW_CHUNK = 64
CONV_COL_CHUNK = 256


def _conv_body(a_ref, g_ref, ah_ref, gh_ref, dw_ref, db_ref, lg_ref, lb_ref, mg_ref, o_ref, hbuf, cbuf, *, width):
    ts, cw = a_ref.shape
    halo = ah_ref.shape[0]
    i = pl.program_id(1)
    hbuf[halo:halo + ts, :] = a_ref[...] * jax.nn.sigmoid(g_ref[...])

    @pl.when(i > 0)
    def _():
        hbuf[0:halo, :] = ah_ref[...] * jax.nn.sigmoid(gh_ref[...])

    @pl.when(i == 0)
    def _():
        hbuf[0:halo, :] = jnp.zeros((halo, cw), F32)

    off = halo - (width - 1)
    rc, cc = min(CONV_ROW_CHUNK, ts), min(CONV_COL_CHUNK, cw)
    for r0 in range(0, ts, rc):
        for c0 in range(0, cw, cc):
            acc = jnp.zeros((rc, cc), F32)
            for j in range(width):
                acc += hbuf[r0 + off + j:r0 + off + j + rc, c0:c0 + cc] * dw_ref[j:j + 1, c0:c0 + cc]
            cbuf[r0:r0 + rc, c0:c0 + cc] = acc + db_ref[:, c0:c0 + cc]

    for r0 in range(0, ts, rc):
        y = cbuf[r0:r0 + rc, :]
        yc = y - jnp.mean(y, axis=-1, keepdims=True)
        yn = yc * lax.rsqrt(jnp.mean(yc * yc, axis=-1, keepdims=True) + EPS) * lg_ref[...] + lb_ref[...]
        z = jax.nn.silu(yn)
        o_ref[r0:r0 + rc, :] = _rms_rows(z, mg_ref[...]).astype(o_ref.dtype)


def _conformer_conv(proj, dw, db, ln_g, ln_b, mix_g, *, bsz, seq, a_col, g_col, ts=256):
    n_cols = proj.shape[1]
    width, cw = dw.shape
    assert n_cols % cw == 0 and width - 1 <= CONV_HALO_ROWS
    ts = _tile(seq, ts)
    assert ts % CONV_HALO_ROWS == 0
    hpt = ts // CONV_HALO_ROWS
    view = proj.reshape(bsz, seq, n_cols)

    def main(col):
        return pl.BlockSpec((None, ts, cw), lambda b, i: (b, i, col))

    def halo(col):
        return pl.BlockSpec((None, CONV_HALO_ROWS, cw), lambda b, i: (b, jnp.maximum(i * hpt - 1, 0), col))

    vec = pl.BlockSpec((1, cw), lambda b, i: (0, 0))
    out = pl.pallas_call(
        functools.partial(_conv_body, width=width),
        grid=(bsz, seq // ts),
        in_specs=[main(a_col), main(g_col), halo(a_col), halo(g_col),
                  pl.BlockSpec((width, cw), lambda b, i: (0, 0)), vec, vec, vec, vec],
        out_specs=pl.BlockSpec((None, ts, cw), lambda b, i: (b, i, 0)),
        out_shape=jax.ShapeDtypeStruct((bsz, seq, cw), BF16),
        scratch_shapes=[pltpu.VMEM((CONV_HALO_ROWS + ts, cw), F32), pltpu.VMEM((ts, cw), F32)],
        compiler_params=_params("parallel", "parallel"),
        name="conformer_conv",
    )(view, view, view, view, dw, db.reshape(1, cw), ln_g.reshape(1, cw), ln_b.reshape(1, cw), mix_g.reshape(1, cw))
    return out.reshape(bsz * seq, cw)


def _ssm_proj_body(x_ref, g_ref, w_ref, o_ref):
    h = _rms_rows(x_ref[...], g_ref[...]).astype(BF16)
    acc = jnp.dot(h, w_ref[...], preferred_element_type=F32)
    for j in range(o_ref.shape[0]):
        o_ref[j] = acc[:, j * V7X_LANES:(j + 1) * V7X_LANES]


def _ssm_proj(x, g, w, *, chunk, tm=512):
    m, k = x.shape
    sw = w.shape[1]
    assert sw % V7X_LANES == 0 and m % chunk == 0
    nj = sw // V7X_LANES
    rows = m // chunk
    tm = _tile(rows, tm)
    return pl.pallas_call(
        _ssm_proj_body,
        grid=(rows // tm, chunk),
        in_specs=[
            pl.BlockSpec((tm, k), lambda i, s: (i, s)),
            pl.BlockSpec((1, k), lambda i, s: (0, 0)),
            pl.BlockSpec((k, sw), lambda i, s: (0, 0)),
        ],
        out_specs=pl.BlockSpec((nj, tm, V7X_LANES), lambda i, s: (0, i, s)),
        out_shape=jax.ShapeDtypeStruct((nj, rows, chunk * V7X_LANES), F32),
        compiler_params=_params("parallel", "parallel"),
        name="ssm_proj",
    )(x.reshape(rows, chunk * k), g.reshape(1, k), w)


def _ssm_chunk_body(u_ref, ki_ref, ks_ref, ko_ref, lt_ref, y_ref, inj_ref, x0_ref):
    nc = u_ref.shape[0]
    half = lt_ref.shape[1] // 2
    u = u_ref[...].astype(BF16)
    inj_ref[...] = jnp.dot(u, ks_ref[...], preferred_element_type=F32)
    lr, li = lt_ref[:, :half], lt_ref[:, half:]

    def step(c, carry):
        xr, xi = carry
        x0_ref[pl.ds(c, 1), :half] = xr
        x0_ref[pl.ds(c, 1), half:] = xi
        row = inj_ref[pl.ds(c, 1), :]
        return lr * xr - li * xi + row[:, :half], lr * xi + li * xr + row[:, half:]

    zero = jnp.zeros((1, half), F32)
    lax.fori_loop(0, nc, step, (zero, zero))
    y = jnp.dot(u, ki_ref[...], preferred_element_type=F32)
    y += jnp.dot(x0_ref[...].astype(BF16), ko_ref[...], preferred_element_type=F32)
    y_ref[...] = y


def _ssm_chunks(u_lay, k_intra, k_state, k_out, lam_t, *, bsz):
    nj, rows, tl = u_lay.shape
    nc = rows // bsz
    st = k_state.shape[2]
    mat = lambda a, b: pl.BlockSpec((None, a, b), lambda j, bb: (j, 0, 0))
    return pl.pallas_call(
        _ssm_chunk_body,
        grid=(nj, bsz),
        in_specs=[
            pl.BlockSpec((None, nc, tl), lambda j, bb: (j, bb, 0)),
            mat(tl, tl), mat(tl, st), mat(st, tl), mat(1, st),
        ],
        out_specs=pl.BlockSpec((None, nc, tl), lambda j, bb: (j, bb, 0)),
        out_shape=jax.ShapeDtypeStruct((nj, rows, tl), F32),
        scratch_shapes=[pltpu.VMEM((nc, st), F32), pltpu.VMEM((nc, st), F32)],
        compiler_params=_params("parallel", "parallel"),
        name="ssm_chunks",
    )(u_lay, k_intra, k_state, k_out, lam_t)


def _ssm_post_body(y_ref, u_ref, d_ref, wg_ref, bg_ref, mg_ref, o_ref):
    nj = y_ref.shape[0]
    y = jnp.concatenate([y_ref[j] for j in range(nj)], axis=-1)
    u = jnp.concatenate([u_ref[j] for j in range(nj)], axis=-1)
    z = jax.nn.gelu(y + d_ref[...] * u)
    gate = jnp.dot(z.astype(BF16), wg_ref[...], preferred_element_type=F32) + bg_ref[...]
    out = z * jax.nn.sigmoid(gate)
    o_ref[...] = _rms_rows(out, mg_ref[...]).astype(o_ref.dtype)


def _ssm_post(y_lay, u_lay, d_skip, w_glu, b_glu, mix_g, *, chunk, tm=512):
    nj, rows, _ = y_lay.shape
    sw = nj * V7X_LANES
    tm = _tile(rows, tm)
    lay = pl.BlockSpec((nj, tm, V7X_LANES), lambda i, s: (0, i, s))
    vec = pl.BlockSpec((1, sw), lambda i, s: (0, 0))
    out = pl.pallas_call(
        _ssm_post_body,
        grid=(rows // tm, chunk),
        in_specs=[lay, lay, vec, pl.BlockSpec((sw, sw), lambda i, s: (0, 0)), vec, vec],
        out_specs=pl.BlockSpec((tm, sw), lambda i, s: (i, s)),
        out_shape=jax.ShapeDtypeStruct((rows, chunk * sw), BF16),
        compiler_params=_params("parallel", "parallel"),
        name="ssm_post",
    )(y_lay, u_lay, d_skip.reshape(1, sw), w_glu, b_glu.reshape(1, sw), mix_g.reshape(1, sw))
    return out.reshape(rows * chunk, sw)


def _ssm_chunk_matrices(a_re, a_im, b_re, b_im, c_re, c_im, log_step, chunk):
    n_groups, n_state = a_re.shape
    n_ch = b_re.shape[-1]
    gpb = V7X_LANES // n_ch
    assert V7X_LANES % n_ch == 0 and n_groups % gpb == 0
    nj = n_groups // gpb
    hi = lax.Precision.HIGHEST

    step = jnp.exp(log_step)[:, None]
    mag, ang = jnp.exp(a_re * step), a_im * step
    lr, li = mag * jnp.cos(ang), mag * jnp.sin(ang)
    den = a_re * a_re + a_im * a_im
    fr = ((lr - 1.0) * a_re + li * a_im) / den
    fi = (li * a_re - (lr - 1.0) * a_im) / den
    bbr = fr[..., None] * b_re - fi[..., None] * b_im
    bbi = fr[..., None] * b_im + fi[..., None] * b_re

    prs, pis = [jnp.ones_like(lr)], [jnp.zeros_like(lr)]
    for _ in range(chunk):
        prs.append(prs[-1] * lr - pis[-1] * li)
        pis.append(prs[-2] * li + pis[-1] * lr)
    pr, pi = jnp.stack(prs), jnp.stack(pis)

    wr = pr[:chunk, :, :, None] * bbr - pi[:chunk, :, :, None] * bbi
    wi = pr[:chunk, :, :, None] * bbi + pi[:chunk, :, :, None] * bbr
    kt = (jnp.einsum("ghp,tgpk->gtkh", c_re, wr, precision=hi)
          - jnp.einsum("ghp,tgpk->gtkh", c_im, wi, precision=hi))
    clr = c_re[None] * pr[1:, :, None, :] - c_im[None] * pi[1:, :, None, :]
    cli = c_re[None] * pi[1:, :, None, :] + c_im[None] * pr[1:, :, None, :]

    eye = jnp.eye(gpb, dtype=F32)
    tl = chunk * V7X_LANES
    st = 2 * gpb * n_state

    lag = jnp.arange(chunk)[None, :] - jnp.arange(chunk)[:, None]
    kst = jnp.where((lag >= 0)[None, :, :, None, None], kt[:, jnp.clip(lag, 0), :, :], 0.0)
    kst = kst.reshape(nj, gpb, chunk, chunk, n_ch, n_ch).transpose(0, 2, 1, 4, 3, 5)
    k_intra = (kst[:, :, :, :, :, None, :] * eye[None, None, :, None, None, :, None]).reshape(nj, tl, tl)

    wrev = jnp.stack([wr[::-1], wi[::-1]])
    wrev = wrev.reshape(2, chunk, nj, gpb, n_state, n_ch).transpose(2, 1, 3, 5, 0, 4)
    k_state = (wrev[:, :, :, :, :, None, :] * eye[None, None, :, None, None, :, None]).reshape(nj, tl, st)

    cl = jnp.stack([clr, -cli])
    cl = cl.reshape(2, chunk, nj, gpb, n_ch, n_state).transpose(2, 0, 3, 5, 1, 4)
    k_out = (cl[:, :, :, :, :, None, :] * eye[None, None, :, None, None, :, None]).reshape(nj, st, tl)

    lam_t = jnp.stack([pr[chunk], pi[chunk]]).reshape(2, nj, gpb * n_state).transpose(1, 0, 2).reshape(nj, 1, st)
    return k_intra.astype(BF16), k_state.astype(BF16), k_out.astype(BF16), lam_t


def _cross_body(x_ref, g_ref, wq_ref, kv_ref, qg_ref, kg_ref, wo_ref, o_ref, *, n_heads, hd):
    x = x_ref[...]
    h = _rms_rows(x, g_ref[...]).astype(BF16)
    q = jnp.dot(h, wq_ref[...], preferred_element_type=F32)
    mw = n_heads * hd
    scale = hd ** -0.5
    heads = []
    for hh in range(n_heads):
        cs = slice(hh * hd, (hh + 1) * hd)
        qn = (_rms_rows(q[:, cs], qg_ref[...]) * scale).astype(BF16)
        kn = _rms_rows(kv_ref[:, cs], kg_ref[...]).astype(BF16)
        v = kv_ref[:, mw + hh * hd:mw + (hh + 1) * hd].astype(BF16)
        s = lax.dot_general(qn, kn, (((1,), (1,)), ((), ())), preferred_element_type=F32)
        e = jnp.exp(s - jnp.max(s, axis=-1, keepdims=True))
        p = e / jnp.sum(e, axis=-1, keepdims=True)
        heads.append(jnp.dot(p.astype(BF16), v, preferred_element_type=F32))
    o = jnp.concatenate(heads, axis=-1).astype(BF16)
    o_ref[...] = x + jnp.dot(o, wo_ref[...], preferred_element_type=F32)


def _cross_attention(x, g, w_cq, kv, cq_g, ck_g, w_co, *, bsz, seq, mem_len, tm=256):
    m, d = x.shape
    mw = w_cq.shape[1]
    hd = cq_g.shape[-1]
    n_heads = mw // hd
    tm = _tile(seq, tm)
    spb = seq // tm
    const = lambda a, b: pl.BlockSpec((a, b), lambda bb, i: (0, 0))
    return pl.pallas_call(
        functools.partial(_cross_body, n_heads=n_heads, hd=hd),
        grid=(bsz, spb),
        in_specs=[
            pl.BlockSpec((tm, d), lambda bb, i: (bb * spb + i, 0)),
            const(1, d), const(d, mw),
            pl.BlockSpec((mem_len, 2 * mw), lambda bb, i: (bb, 0)),
            const(1, hd), const(1, hd), const(mw, d),
        ],
        out_specs=pl.BlockSpec((tm, d), lambda bb, i: (bb * spb + i, 0)),
        out_shape=jax.ShapeDtypeStruct((m, d), F32),
        compiler_params=_params("parallel", "parallel"),
        name="cross_attn",
    )(x, g.reshape(1, d), w_cq, kv, cq_g.reshape(1, hd), ck_g.reshape(1, hd), w_co)


def kernel(x, mem, norm_mix, w_in, q_norm, k_norm, conv_dw, conv_b, conv_ln_g, conv_ln_b, ssm_a_re, ssm_a_im, ssm_b_re, ssm_b_im, ssm_c_re, ssm_c_im, ssm_d, ssm_log_step, ssm_w_glu, ssm_b_glu, mix_out_norm, w_out, norm_cross, norm_mem, w_cq, w_ckv, cq_norm, ck_norm, w_co, norm_mlp, w_up, w_down):
    bsz, seq, d_model = x.shape
    mem_len = mem.shape[1]
    depth = w_in.shape[0]
    hd = q_norm.shape[-1]
    conv_w = conv_b.shape[-1]
    ssm_w = ssm_d.shape[-1]
    attn_w = w_out.shape[1] - conv_w - ssm_w
    n_heads = attn_w // hd
    main_w = 3 * attn_w + 2 * conv_w
    assert main_w + ssm_w == w_in.shape[-1]
    assert main_w % attn_w == 0 and (3 * attn_w) % conv_w == 0

    xf = x.reshape(bsz * seq, d_model)
    memf = mem.reshape(bsz * mem_len, d_model)
    for l in range(depth):
        w_main = w_in[l, :, :main_w].astype(BF16)
        w_ssm = w_in[l, :, main_w:].astype(BF16)
        mix_g = mix_out_norm[l]

        proj = _norm_matmul(xf, norm_mix[l], w_main)
        u_lay = _ssm_proj(xf, norm_mix[l], w_ssm, chunk=SSM_CHUNK)

        outs, lses = [], []
        for window, dilation in DILATION_PATTERNS:
            o, lse = _dilated_attention(proj, q_norm[l], k_norm[l], bsz=bsz, seq=seq, n_heads=n_heads, hd=hd,
                                        window=window, dilation=dilation)
            outs.append(o)
            lses.append(lse)
        attn_n = _attn_merge(outs, lses, mix_g[:attn_w], n_heads=n_heads, hd=hd)

        conv_n = _conformer_conv(proj, conv_dw[l].reshape(conv_dw.shape[1], conv_w), conv_b[l], conv_ln_g[l],
                                 conv_ln_b[l], mix_g[attn_w:attn_w + conv_w], bsz=bsz, seq=seq,
                                 a_col=3 * attn_w // conv_w, g_col=3 * attn_w // conv_w + 1)

        k_intra, k_state, k_out, lam_t = _ssm_chunk_matrices(
            ssm_a_re[l], ssm_a_im[l], ssm_b_re[l], ssm_b_im[l], ssm_c_re[l], ssm_c_im[l], ssm_log_step[l], SSM_CHUNK)
        y_lay = _ssm_chunks(u_lay, k_intra, k_state, k_out, lam_t, bsz=bsz)
        ssm_n = _ssm_post(y_lay, u_lay, ssm_d[l], ssm_w_glu[l].astype(BF16), ssm_b_glu[l],
                          mix_g[attn_w + conv_w:], chunk=SSM_CHUNK)

        xf = _out_proj(attn_n, conv_n, ssm_n, w_out[l].astype(BF16), xf)

        kv = _norm_matmul(memf, norm_mem[l], w_ckv[l].astype(BF16))
        xf = _cross_attention(xf, norm_cross[l], w_cq[l].astype(BF16), kv, cq_norm[l], ck_norm[l],
                              w_co[l].astype(BF16), bsz=bsz, seq=seq, mem_len=mem_len)

        hid = _norm_matmul(xf, norm_mlp[l], w_up[l].astype(BF16), act="relu2", out_dtype=BF16)
        xf = _matmul_res(hid, w_down[l].astype(BF16), xf)
    return xf.reshape(bsz, seq, d_model)
```

```python
import functools

import jax
import jax.numpy as jnp
from jax import lax
from jax.experimental import pallas as pl
from jax.experimental.pallas import tpu as pltpu

F32 = jnp.float32
BF16 = jnp.bfloat16

EPS = 1e-6
NEG_INF = -1e30
DILATION_PATTERNS = ((128, 1), (512, 4), (2048, 16))

V7X_LANES = 128
V7X_VMEM_LIMIT_BYTES = 60000 * 1024
SSM_CHUNK = 16


def _tile(n, pref, align=8):
    if n <= pref:
        return n
    for t in range(pref - pref % align, 0, -align):
        if n % t == 0:
            return t
    raise ValueError((n, pref, align))


def _params(*semantics):
    return pltpu.CompilerParams(dimension_semantics=semantics, vmem_limit_bytes=V7X_VMEM_LIMIT_BYTES)


def _rms_rows(x, g):
    return x * lax.rsqrt(jnp.mean(x * x, axis=-1, keepdims=True) + EPS) * g


def _norm_body(x_ref, g_ref, o_ref):
    o_ref[...] = _rms_rows(x_ref[...], g_ref[...]).astype(o_ref.dtype)


def _rms_norm_bf16(x, g, *, tm=256):
    m, k = x.shape
    tm = _tile(m, tm)
    return pl.pallas_call(
        _norm_body,
        grid=(m // tm,),
        in_specs=[pl.BlockSpec((tm, k), lambda i: (i, 0)), pl.BlockSpec((1, k), lambda i: (0, 0))],
        out_specs=pl.BlockSpec((tm, k), lambda i: (i, 0)),
        out_shape=jax.ShapeDtypeStruct((m, k), BF16),
        compiler_params=_params("parallel"),
        name="rms_norm",
    )(x, g.reshape(1, k))


WEIGHT_CAST_ROWS = 512


def _cast_panel(w_ref, wb_ref):
    rows = wb_ref.shape[0]
    rc = _tile(rows, WEIGHT_CAST_ROWS)

    def chunk(c, carry):
        sl = pl.ds(pl.multiple_of(c * rc, rc), rc)
        wb_ref[sl, :] = w_ref[sl, :].astype(wb_ref.dtype)
        return carry

    lax.fori_loop(0, rows // rc, chunk, 0)


def _panel_matmul_body(*refs, n_lhs, act, has_res):
    a_refs = refs[:n_lhs]
    w_ref = refs[n_lhs]
    r_ref = refs[n_lhs + 1] if has_res else None
    o_ref, wb_ref = refs[-2], refs[-1]

    @pl.when(pl.program_id(1) == 0)
    def _():
        _cast_panel(w_ref, wb_ref)

    acc = None
    k0 = 0
    for a_ref in a_refs:
        kk = a_ref.shape[1]
        part = jnp.dot(a_ref[...], wb_ref[k0:k0 + kk, :], preferred_element_type=F32)
        acc = part if acc is None else acc + part
        k0 += kk
    if act == "relu2":
        acc = jnp.square(jnp.maximum(acc, 0.0))
    if has_res:
        acc = r_ref[...] + acc
    o_ref[...] = acc.astype(o_ref.dtype)


def _panel_matmul(lhs, w, layer, *, res=None, act=None, out_dtype=F32, tm=1024, tn=512):
    m = lhs[0].shape[0]
    k, n = w.shape[1], w.shape[2]
    assert sum(a.shape[1] for a in lhs) == k
    tm, tn = _tile(m, tm), _tile(n, tn, V7X_LANES)
    in_specs = [pl.BlockSpec((tm, a.shape[1]), lambda j, i: (i, 0)) for a in lhs]
    in_specs.append(pl.BlockSpec((None, k, tn), lambda j, i: (layer, 0, j)))
    args = list(lhs) + [w]
    if res is not None:
        in_specs.append(pl.BlockSpec((tm, tn), lambda j, i: (i, j)))
        args.append(res)
    return pl.pallas_call(
        functools.partial(_panel_matmul_body, n_lhs=len(lhs), act=act, has_res=res is not None),
        grid=(n // tn, m // tm),
        in_specs=in_specs,
        out_specs=pl.BlockSpec((tm, tn), lambda j, i: (i, j)),
        out_shape=jax.ShapeDtypeStruct((m, n), out_dtype),
        scratch_shapes=[pltpu.VMEM((k, tn), BF16)],
        compiler_params=_params("parallel", "arbitrary"),
        name="panel_matmul",
    )(*args)


def _matmul_res_body(a_ref, w_ref, r_ref, o_ref, acc_ref):
    kk = pl.program_id(2)

    @pl.when(kk == 0)
    def _():
        acc_ref[...] = jnp.zeros_like(acc_ref)

    acc_ref[...] += jnp.dot(a_ref[...], w_ref[...].astype(BF16), preferred_element_type=F32)

    @pl.when(kk == pl.num_programs(2) - 1)
    def _():
        o_ref[...] = r_ref[...] + acc_ref[...]


def _matmul_res(a, w, layer, res, *, tm=1024, tn=1024, tk=2048):
    m, k = a.shape
    n = w.shape[2]
    tm, tn, tk = _tile(m, tm), _tile(n, tn, V7X_LANES), _tile(k, tk, V7X_LANES)
    return pl.pallas_call(
        _matmul_res_body,
        grid=(m // tm, n // tn, k // tk),
        in_specs=[
            pl.BlockSpec((tm, tk), lambda i, j, kk: (i, kk)),
            pl.BlockSpec((None, tk, tn), lambda i, j, kk: (layer, kk, j)),
            pl.BlockSpec((tm, tn), lambda i, j, kk: (i, j)),
        ],
        out_specs=pl.BlockSpec((tm, tn), lambda i, j, kk: (i, j)),
        out_shape=jax.ShapeDtypeStruct((m, n), F32),
        scratch_shapes=[pltpu.VMEM((tm, tn), F32)],
        compiler_params=_params("parallel", "parallel", "arbitrary"),
        name="matmul_res",
    )(a, w, res)


def _norm_matmul_body(x_ref, g_ref, w_ref, o_ref):
    h = _rms_rows(x_ref[...], g_ref[...]).astype(BF16)
    o_ref[...] = jnp.dot(h, w_ref[...], preferred_element_type=F32)


def _norm_matmul(x, g, w, *, tm=512):
    m, k = x.shape
    n = w.shape[1]
    tm = _tile(m, tm)
    return pl.pallas_call(
        _norm_matmul_body,
        grid=(m // tm,),
        in_specs=[
            pl.BlockSpec((tm, k), lambda i: (i, 0)),
            pl.BlockSpec((1, k), lambda i: (0, 0)),
            pl.BlockSpec((k, n), lambda i: (0, 0)),
        ],
        out_specs=pl.BlockSpec((tm, n), lambda i: (i, 0)),
        out_shape=jax.ShapeDtypeStruct((m, n), F32),
        compiler_params=_params("parallel"),
        name="norm_matmul",
    )(x, g.reshape(1, k), w)


ATTN_NORM_ROWS = 256
ATTN_BLOCKS_PER_ITER = 4


def _attn_body(q_ref, k_ref, v_ref, qg_ref, kg_ref, o_ref, qn_ref, kn_ref, acc_ref, m_ref, l_ref,
               *, dilations, blk, scale):
    seq, hd = q_ref.shape
    qg, kg = qg_ref[...], kg_ref[...]
    rc = _tile(seq, ATTN_NORM_ROWS)

    def norm_chunk(c, carry):
        rows = pl.ds(pl.multiple_of(c * rc, rc), rc)
        qn_ref[rows, :] = _rms_rows(q_ref[rows, :], qg) * scale
        kn_ref[rows, :] = _rms_rows(k_ref[rows, :], kg)
        return carry

    lax.fori_loop(0, seq // rc, norm_chunk, 0)

    rel = (lax.broadcasted_iota(jnp.int32, (blk, 2 * blk), 0)
           - lax.broadcasted_iota(jnp.int32, (blk, 2 * blk), 1))

    def rows_of(start, size, d):
        if d == 1:
            return pl.ds(pl.multiple_of(start, blk), size)
        return pl.ds(start, size, stride=d)

    for pi, d in enumerate(dilations):
        nb = seq // (d * blk)
        first = pi == 0

        def one_block(idx, d=d, nb=nb, first=first):
            r, n = idx // nb, idx % nb
            kb = jnp.maximum(n - 1, 0)
            dist = rel + (n - kb) * blk
            q_rows = rows_of(n * (blk * d) + r, blk, d)
            k_rows = rows_of(kb * (blk * d) + r, 2 * blk, d)
            q = qn_ref[q_rows, :].astype(BF16)
            kk = kn_ref[k_rows, :].astype(BF16)
            vv = v_ref[k_rows, :].astype(BF16)
            s = lax.dot_general(q, kk, (((1,), (1,)), ((), ())), preferred_element_type=F32)
            s = jnp.where(dist >= 0, jnp.where(dist <= blk, s, NEG_INF), NEG_INF)
            m_b = jnp.max(s, axis=-1, keepdims=True)
            p = jnp.exp(s - m_b)
            l_b = jnp.sum(p, axis=-1, keepdims=True)
            o_b = jnp.dot(p.astype(BF16), vv, preferred_element_type=F32)
            if first:
                m_ref[q_rows, :] = jnp.broadcast_to(m_b, (blk, hd))
                l_ref[q_rows, :] = jnp.broadcast_to(l_b, (blk, hd))
                acc_ref[q_rows, :] = o_b
            else:
                m_o = m_ref[q_rows, :]
                m_n = jnp.maximum(m_o, m_b)
                alpha = jnp.exp(m_o - m_n)
                beta = jnp.exp(m_b - m_n)
                m_ref[q_rows, :] = m_n
                l_ref[q_rows, :] = alpha * l_ref[q_rows, :] + beta * l_b
                acc_ref[q_rows, :] = alpha * acc_ref[q_rows, :] + beta * o_b

        n_blocks = seq // blk
        per_iter = ATTN_BLOCKS_PER_ITER if n_blocks % ATTN_BLOCKS_PER_ITER == 0 else 1

        def block_group(g, carry, one_block=one_block, per_iter=per_iter):
            for u in range(per_iter):
                one_block(g * per_iter + u)
            return carry

        lax.fori_loop(0, n_blocks // per_iter, block_group, 0)

    def out_chunk(c, carry):
        rows = pl.ds(pl.multiple_of(c * rc, rc), rc)
        o_ref[rows, :] = acc_ref[rows, :] / l_ref[rows, :]
        return carry

    lax.fori_loop(0, seq // rc, out_chunk, 0)


def _dilated_attention(proj, q_g, k_g, *, bsz, seq, n_heads, hd, k_col, v_col):
    assert hd == V7X_LANES
    blk = DILATION_PATTERNS[0][0] // DILATION_PATTERNS[0][1]
    for window, d in DILATION_PATTERNS:
        assert window == blk * d and seq % (d * blk) == 0 and seq // (d * blk) >= 2
    dilations = tuple(sorted((d for _, d in DILATION_PATTERNS), reverse=True))
    head = lambda off: pl.BlockSpec((seq, hd), lambda b, h: (b, off + h))
    gain = pl.BlockSpec((1, hd), lambda b, h: (0, 0))
    return pl.pallas_call(
        functools.partial(_attn_body, dilations=dilations, blk=blk, scale=hd ** -0.5),
        grid=(bsz, n_heads),
        in_specs=[head(0), head(k_col), head(v_col), gain, gain],
        out_specs=head(0),
        out_shape=jax.ShapeDtypeStruct((bsz * seq, n_heads * hd), F32),
        scratch_shapes=[pltpu.VMEM((seq, hd), F32)] * 5,
        compiler_params=_params("parallel", "parallel"),
        name="dilated_attn",
    )(proj, proj, proj, q_g.reshape(1, hd), k_g.reshape(1, hd))


CONV_HALO_ROWS = 32
CONV_ROW_CHUNK = 64
CONV_COL_CHUNK = 256


def _conv_body(a_ref, g_ref, ah_ref, gh_ref, dw_ref, db_ref, lg_ref, lb_ref, mg_ref, o_ref, hbuf, cbuf, *, width):
    ts, cw = a_ref.shape
    halo = ah_ref.shape[0]
    i = pl.program_id(1)
    hbuf[halo:halo + ts, :] = a_ref[...] * jax.nn.sigmoid(g_ref[...])

    @pl.when(i > 0)
    def _():
        hbuf[0:halo, :] = ah_ref[...] * jax.nn.sigmoid(gh_ref[...])

    @pl.when(i == 0)
    def _():
        hbuf[0:halo, :] = jnp.zeros((halo, cw), F32)

    off = halo - (width - 1)
    rc, cc = min(CONV_ROW_CHUNK, ts), min(CONV_COL_CHUNK, cw)
    for r0 in range(0, ts, rc):
        for c0 in range(0, cw, cc):
            acc = jnp.zeros((rc, cc), F32)
            for j in range(width):
                acc += hbuf[r0 + off + j:r0 + off + j + rc, c0:c0 + cc] * dw_ref[j:j + 1, c0:c0 + cc]
            cbuf[r0:r0 + rc, c0:c0 + cc] = acc + db_ref[:, c0:c0 + cc]

    for r0 in range(0, ts, rc):
        y = cbuf[r0:r0 + rc, :]
        yc = y - jnp.mean(y, axis=-1, keepdims=True)
        yn = yc * lax.rsqrt(jnp.mean(yc * yc, axis=-1, keepdims=True) + EPS) * lg_ref[...] + lb_ref[...]
        z = jax.nn.silu(yn)
        o_ref[r0:r0 + rc, :] = _rms_rows(z, mg_ref[...]).astype(o_ref.dtype)


def _conformer_conv(proj, dw, db, ln_g, ln_b, mix_g, *, bsz, seq, a_col, g_col, ts=256):
    n_cols = proj.shape[1]
    width, cw = dw.shape
    assert width - 1 <= CONV_HALO_ROWS
    ts = _tile(seq, ts)
    assert ts % CONV_HALO_ROWS == 0
    hpt = ts // CONV_HALO_ROWS
    view = proj.reshape(bsz, seq, n_cols)

    def main(col):
        return pl.BlockSpec((None, ts, cw), lambda b, i: (b, i, col))

    def halo(col):
        return pl.BlockSpec((None, CONV_HALO_ROWS, cw), lambda b, i: (b, jnp.maximum(i * hpt - 1, 0), col))

    vec = pl.BlockSpec((1, cw), lambda b, i: (0, 0))
    out = pl.pallas_call(
        functools.partial(_conv_body, width=width),
        grid=(bsz, seq // ts),
        in_specs=[main(a_col), main(g_col), halo(a_col), halo(g_col),
                  pl.BlockSpec((width, cw), lambda b, i: (0, 0)), vec, vec, vec, vec],
        out_specs=pl.BlockSpec((None, ts, cw), lambda b, i: (b, i, 0)),
        out_shape=jax.ShapeDtypeStruct((bsz, seq, cw), BF16),
        scratch_shapes=[pltpu.VMEM((CONV_HALO_ROWS + ts, cw), F32), pltpu.VMEM((ts, cw), F32)],
        compiler_params=_params("parallel", "parallel"),
        name="conformer_conv",
    )(view, view, view, view, dw, db.reshape(1, cw), ln_g.reshape(1, cw), ln_b.reshape(1, cw), mix_g.reshape(1, cw))
    return out.reshape(bsz * seq, cw)


def _ssm_chunk_body(u_ref, d_ref, ks_ref, ko_ref, lt_ref, y_ref, kin_ref, ul_ref, inj_ref, x0_ref, *, chunk):
    seq, lanes = u_ref.shape
    nc = seq // chunk
    half = lt_ref.shape[1] // 2

    @pl.when(pl.program_id(1) == 0)
    def _():
        zero = jnp.zeros((lanes, lanes), kin_ref.dtype)
        for s in range(chunk):
            for t in range(chunk):
                kin_ref[s * lanes:(s + 1) * lanes, t * lanes:(t + 1) * lanes] = d_ref[t - s] if t >= s else zero

    for s in range(chunk):
        ul_ref[:, s * lanes:(s + 1) * lanes] = u_ref[pl.ds(s, nc, stride=chunk), :].astype(ul_ref.dtype)
    u = ul_ref[...]
    inj_ref[...] = jnp.dot(u, ks_ref[...], preferred_element_type=F32)
    lr, li = lt_ref[:, :half], lt_ref[:, half:]

    def step(c, carry):
        xr, xi = carry
        x0_ref[pl.ds(c, 1), :half] = xr
        x0_ref[pl.ds(c, 1), half:] = xi
        row = inj_ref[pl.ds(c, 1), :]
        return lr * xr - li * xi + row[:, :half], lr * xi + li * xr + row[:, half:]

    zero = jnp.zeros((1, half), F32)
    lax.fori_loop(0, nc, step, (zero, zero))
    y = jnp.dot(u, kin_ref[...], preferred_element_type=F32)
    y += jnp.dot(x0_ref[...].astype(BF16), ko_ref[...], preferred_element_type=F32)
    for t in range(chunk):
        y_ref[pl.ds(t, nc, stride=chunk), :] = y[:, t * lanes:(t + 1) * lanes]


def _ssm_chunks(proj, lag_blocks, k_state, k_out, lam_t, *, bsz, seq, u_col, chunk):
    nj, _, lanes, _ = lag_blocks.shape
    tl = chunk * lanes
    st = k_state.shape[2]
    nc = seq // chunk
    mat = lambda a, b: pl.BlockSpec((None, a, b), lambda j, bb: (j, 0, 0))
    return pl.pallas_call(
        functools.partial(_ssm_chunk_body, chunk=chunk),
        grid=(nj, bsz),
        in_specs=[
            pl.BlockSpec((seq, lanes), lambda j, bb: (bb, u_col + j)),
            pl.BlockSpec((None, chunk, lanes, lanes), lambda j, bb: (j, 0, 0, 0)),
            mat(tl, st), mat(st, tl), mat(1, st),
        ],
        out_specs=pl.BlockSpec((seq, lanes), lambda j, bb: (bb, j)),
        out_shape=jax.ShapeDtypeStruct((bsz * seq, nj * lanes), F32),
        scratch_shapes=[pltpu.VMEM((tl, tl), BF16), pltpu.VMEM((nc, tl), BF16),
                        pltpu.VMEM((nc, st), F32), pltpu.VMEM((nc, st), F32)],
        compiler_params=_params("parallel", "arbitrary"),
        name="ssm_chunks",
    )(proj, lag_blocks, k_state, k_out, lam_t)


def _ssm_post_body(y_ref, u_ref, d_ref, wg_ref, bg_ref, mg_ref, o_ref):
    z = jax.nn.gelu(y_ref[...] + d_ref[...] * u_ref[...])
    gate = jnp.dot(z.astype(BF16), wg_ref[...], preferred_element_type=F32) + bg_ref[...]
    out = z * jax.nn.sigmoid(gate)
    o_ref[...] = _rms_rows(out, mg_ref[...]).astype(o_ref.dtype)


def _ssm_post(y, proj, d_skip, w_glu, b_glu, mix_g, *, u_col, tm=512):
    m, sw = y.shape
    tm = _tile(m, tm)
    vec = pl.BlockSpec((1, sw), lambda i: (0, 0))
    return pl.pallas_call(
        _ssm_post_body,
        grid=(m // tm,),
        in_specs=[pl.BlockSpec((tm, sw), lambda i: (i, 0)), pl.BlockSpec((tm, sw), lambda i: (i, u_col)),
                  vec, pl.BlockSpec((sw, sw), lambda i: (0, 0)), vec, vec],
        out_specs=pl.BlockSpec((tm, sw), lambda i: (i, 0)),
        out_shape=jax.ShapeDtypeStruct((m, sw), BF16),
        compiler_params=_params("parallel"),
        name="ssm_post",
    )(y, proj, d_skip.reshape(1, sw), w_glu, b_glu.reshape(1, sw), mix_g.reshape(1, sw))


def _ssm_chunk_matrices(a_re, a_im, b_re, b_im, c_re, c_im, log_step, chunk):
    n_groups, n_state = a_re.shape
    n_ch = b_re.shape[-1]
    gpb = V7X_LANES // n_ch
    assert V7X_LANES % n_ch == 0 and n_groups % gpb == 0
    nj = n_groups // gpb
    hi = lax.Precision.HIGHEST

    step = jnp.exp(log_step)[:, None]
    mag, ang = jnp.exp(a_re * step), a_im * step
    lr, li = mag * jnp.cos(ang), mag * jnp.sin(ang)
    den = a_re * a_re + a_im * a_im
    fr = ((lr - 1.0) * a_re + li * a_im) / den
    fi = (li * a_re - (lr - 1.0) * a_im) / den
    bbr = fr[..., None] * b_re - fi[..., None] * b_im
    bbi = fr[..., None] * b_im + fi[..., None] * b_re

    prs, pis = [jnp.ones_like(lr)], [jnp.zeros_like(lr)]
    for _ in range(chunk):
        prs.append(prs[-1] * lr - pis[-1] * li)
        pis.append(prs[-2] * li + pis[-1] * lr)
    pr, pi = jnp.stack(prs), jnp.stack(pis)

    wr = pr[:chunk, :, :, None] * bbr - pi[:chunk, :, :, None] * bbi
    wi = pr[:chunk, :, :, None] * bbi + pi[:chunk, :, :, None] * bbr
    kt = (jnp.einsum("ghp,tgpk->gtkh", c_re, wr, precision=hi)
          - jnp.einsum("ghp,tgpk->gtkh", c_im, wi, precision=hi))
    clr = c_re[None] * pr[1:, :, None, :] - c_im[None] * pi[1:, :, None, :]
    cli = c_re[None] * pi[1:, :, None, :] + c_im[None] * pr[1:, :, None, :]

    eye = jnp.eye(gpb, dtype=F32)
    tl = chunk * V7X_LANES
    st = 2 * gpb * n_state

    ktj = kt.reshape(nj, gpb, chunk, n_ch, n_ch).transpose(0, 2, 1, 3, 4)
    lag_blocks = (ktj[:, :, :, :, None, :] * eye[None, None, :, None, :, None]).reshape(nj, chunk, V7X_LANES, V7X_LANES)

    wrev = jnp.stack([wr[::-1], wi[::-1]])
    wrev = wrev.reshape(2, chunk, nj, gpb, n_state, n_ch).transpose(2, 1, 3, 5, 0, 4)
    k_state = (wrev[:, :, :, :, :, None, :] * eye[None, None, :, None, None, :, None]).reshape(nj, tl, st)

    cl = jnp.stack([clr, -cli])
    cl = cl.reshape(2, chunk, nj, gpb, n_ch, n_state).transpose(2, 0, 3, 5, 1, 4)
    k_out = (cl[:, :, :, :, :, None, :] * eye[None, None, :, None, None, :, None]).reshape(nj, st, tl)

    lam_t = jnp.stack([pr[chunk], pi[chunk]]).reshape(2, nj, gpb * n_state).transpose(1, 0, 2).reshape(nj, 1, st)
    return lag_blocks.astype(BF16), k_state.astype(BF16), k_out.astype(BF16), lam_t


def _cross_body(x_ref, g_ref, wq_ref, kv_ref, qg_ref, kg_ref, wo_ref, o_ref, *, n_heads, hd):
    x = x_ref[...]
    h = _rms_rows(x, g_ref[...]).astype(BF16)
    q = jnp.dot(h, wq_ref[...], preferred_element_type=F32)
    mw = n_heads * hd
    scale = hd ** -0.5
    heads = []
    for hh in range(n_heads):
        cs = slice(hh * hd, (hh + 1) * hd)
        qn = (_rms_rows(q[:, cs], qg_ref[...]) * scale).astype(BF16)
        kn = _rms_rows(kv_ref[:, cs], kg_ref[...]).astype(BF16)
        v = kv_ref[:, mw + hh * hd:mw + (hh + 1) * hd].astype(BF16)
        s = lax.dot_general(qn, kn, (((1,), (1,)), ((), ())), preferred_element_type=F32)
        e = jnp.exp(s - jnp.max(s, axis=-1, keepdims=True))
        p = e / jnp.sum(e, axis=-1, keepdims=True)
        heads.append(jnp.dot(p.astype(BF16), v, preferred_element_type=F32))
    o = jnp.concatenate(heads, axis=-1).astype(BF16)
    o_ref[...] = x + jnp.dot(o, wo_ref[...], preferred_element_type=F32)


def _cross_attention(x, g, w_cq, kv, cq_g, ck_g, w_co, *, bsz, seq, mem_len, tm=256):
    m, d = x.shape
    mw = w_cq.shape[1]
    hd = cq_g.shape[-1]
    n_heads = mw // hd
    tm = _tile(seq, tm)
    spb = seq // tm
    const = lambda a, b: pl.BlockSpec((a, b), lambda bb, i: (0, 0))
    return pl.pallas_call(
        functools.partial(_cross_body, n_heads=n_heads, hd=hd),
        grid=(bsz, spb),
        in_specs=[
            pl.BlockSpec((tm, d), lambda bb, i: (bb * spb + i, 0)),
            const(1, d), const(d, mw),
            pl.BlockSpec((mem_len, 2 * mw), lambda bb, i: (bb, 0)),
            const(1, hd), const(1, hd), const(mw, d),
        ],
        out_specs=pl.BlockSpec((tm, d), lambda bb, i: (bb * spb + i, 0)),
        out_shape=jax.ShapeDtypeStruct((m, d), F32),
        compiler_params=_params("parallel", "parallel"),
        name="cross_attn",
    )(x, g.reshape(1, d), w_cq, kv, cq_g.reshape(1, hd), ck_g.reshape(1, hd), w_co)


def kernel(x, mem, norm_mix, w_in, q_norm, k_norm, conv_dw, conv_b, conv_ln_g, conv_ln_b, ssm_a_re, ssm_a_im, ssm_b_re, ssm_b_im, ssm_c_re, ssm_c_im, ssm_d, ssm_log_step, ssm_w_glu, ssm_b_glu, mix_out_norm, w_out, norm_cross, norm_mem, w_cq, w_ckv, cq_norm, ck_norm, w_co, norm_mlp, w_up, w_down):
    bsz, seq, d_model = x.shape
    mem_len = mem.shape[1]
    depth = w_in.shape[0]
    hd = q_norm.shape[-1]
    conv_w = conv_b.shape[-1]
    ssm_w = ssm_d.shape[-1]
    attn_w = w_out.shape[1] - conv_w - ssm_w
    n_heads = attn_w // hd
    conv_off = 3 * attn_w
    ssm_off = conv_off + 2 * conv_w
    assert ssm_off + ssm_w == w_in.shape[-1]
    assert conv_off % conv_w == 0 and ssm_off % ssm_w == 0 and ssm_w % V7X_LANES == 0

    xf = x.reshape(bsz * seq, d_model)
    memf = mem.reshape(bsz * mem_len, d_model)
    for l in range(depth):
        mix_g = mix_out_norm[l]

        proj = _panel_matmul([_rms_norm_bf16(xf, norm_mix[l])], w_in, l)

        attn = _dilated_attention(proj, q_norm[l], k_norm[l], bsz=bsz, seq=seq, n_heads=n_heads, hd=hd,
                                  k_col=n_heads, v_col=2 * n_heads)
        attn_n = _rms_norm_bf16(attn, mix_g[:attn_w])

        conv_n = _conformer_conv(proj, conv_dw[l].reshape(conv_dw.shape[1], conv_w), conv_b[l], conv_ln_g[l],
                                 conv_ln_b[l], mix_g[attn_w:attn_w + conv_w], bsz=bsz, seq=seq,
                                 a_col=conv_off // conv_w, g_col=conv_off // conv_w + 1)

        lag_blocks, k_state, k_out, lam_t = _ssm_chunk_matrices(
            ssm_a_re[l], ssm_a_im[l], ssm_b_re[l], ssm_b_im[l], ssm_c_re[l], ssm_c_im[l], ssm_log_step[l], SSM_CHUNK)
        y = _ssm_chunks(proj, lag_blocks, k_state, k_out, lam_t, bsz=bsz, seq=seq,
                        u_col=ssm_off // V7X_LANES, chunk=SSM_CHUNK)
        ssm_n = _ssm_post(y, proj, ssm_d[l], ssm_w_glu[l].astype(BF16), ssm_b_glu[l], mix_g[attn_w + conv_w:],
                          u_col=ssm_off // ssm_w)

        xf = _panel_matmul([attn_n, conv_n, ssm_n], w_out, l, res=xf)

        kv = _norm_matmul(memf, norm_mem[l], w_ckv[l].astype(BF16))
        xf = _cross_attention(xf, norm_cross[l], w_cq[l].astype(BF16), kv, cq_norm[l], ck_norm[l],
                              w_co[l].astype(BF16), bsz=bsz, seq=seq, mem_len=mem_len)

        hid = _panel_matmul([_rms_norm_bf16(xf, norm_mlp[l])], w_up, l, act="relu2", out_dtype=BF16)
        xf = _matmul_res(hid, w_down, l, xf)
    return xf.reshape(bsz, seq, d_model)
```

```python
import functools

import jax
import jax.numpy as jnp
from jax import lax
from jax.experimental import pallas as pl
from jax.experimental.pallas import tpu as pltpu

F32 = jnp.float32
BF16 = jnp.bfloat16

EPS = 1e-6
NEG_INF = -1e30
LOG2_E = 1.4426950408889634
DILATION_PATTERNS = ((128, 1), (512, 4), (2048, 16))

V7X_LANES = 128
V7X_SUBLANES = 8
V7X_VMEM_LIMIT_BYTES = 60000 * 1024
SSM_CHUNK = 16


def _tile(n, pref, align=8):
    if n <= pref:
        return n
    for t in range(pref - pref % align, 0, -align):
        if n % t == 0:
            return t
    raise ValueError((n, pref, align))


def _params(*semantics):
    return pltpu.CompilerParams(dimension_semantics=semantics, vmem_limit_bytes=V7X_VMEM_LIMIT_BYTES)


def _rms_rows(x, g):
    return x * lax.rsqrt(jnp.mean(x * x, axis=-1, keepdims=True) + EPS) * g


def _norm_body(x_ref, g_ref, o_ref):
    o_ref[...] = _rms_rows(x_ref[...], g_ref[...]).astype(o_ref.dtype)


def _rms_norm_bf16(x, g, *, tm=256):
    m, k = x.shape
    tm = _tile(m, tm)
    return pl.pallas_call(
        _norm_body,
        grid=(m // tm,),
        in_specs=[pl.BlockSpec((tm, k), lambda i: (i, 0)), pl.BlockSpec((1, k), lambda i: (0, 0))],
        out_specs=pl.BlockSpec((tm, k), lambda i: (i, 0)),
        out_shape=jax.ShapeDtypeStruct((m, k), BF16),
        compiler_params=_params("parallel"),
        name="rms_norm",
    )(x, g.reshape(1, k))


WEIGHT_CAST_ROWS = 512


def _cast_panel(w_ref, wb_ref):
    rows = wb_ref.shape[0]
    rc = _tile(rows, WEIGHT_CAST_ROWS)

    def chunk(c, carry):
        sl = pl.ds(pl.multiple_of(c * rc, rc), rc)
        wb_ref[sl, :] = w_ref[sl, :].astype(wb_ref.dtype)
        return carry

    lax.fori_loop(0, rows // rc, chunk, 0)


def _panel_matmul_body(*refs, n_lhs, act, has_res):
    a_refs = refs[:n_lhs]
    w_ref = refs[n_lhs]
    r_ref = refs[n_lhs + 1] if has_res else None
    o_ref, wb_ref = refs[-2], refs[-1]

    @pl.when(pl.program_id(1) == 0)
    def _():
        _cast_panel(w_ref, wb_ref)

    acc = None
    k0 = 0
    for a_ref in a_refs:
        kk = a_ref.shape[1]
        part = jnp.dot(a_ref[...], wb_ref[k0:k0 + kk, :], preferred_element_type=F32)
        acc = part if acc is None else acc + part
        k0 += kk
    if act == "relu2":
        acc = jnp.square(jnp.maximum(acc, 0.0))
    if has_res:
        acc = r_ref[...] + acc
    o_ref[...] = acc.astype(o_ref.dtype)


def _panel_matmul(lhs, w, layer, *, res=None, act=None, out_dtype=F32, tm=1024, tn=512):
    m = lhs[0].shape[0]
    k, n = w.shape[1], w.shape[2]
    assert sum(a.shape[1] for a in lhs) == k
    tm, tn = _tile(m, tm), _tile(n, tn, V7X_LANES)
    in_specs = [pl.BlockSpec((tm, a.shape[1]), lambda j, i: (i, 0)) for a in lhs]
    in_specs.append(pl.BlockSpec((None, k, tn), lambda j, i: (layer, 0, j)))
    args = list(lhs) + [w]
    if res is not None:
        in_specs.append(pl.BlockSpec((tm, tn), lambda j, i: (i, j)))
        args.append(res)
    return pl.pallas_call(
        functools.partial(_panel_matmul_body, n_lhs=len(lhs), act=act, has_res=res is not None),
        grid=(n // tn, m // tm),
        in_specs=in_specs,
        out_specs=pl.BlockSpec((tm, tn), lambda j, i: (i, j)),
        out_shape=jax.ShapeDtypeStruct((m, n), out_dtype),
        scratch_shapes=[pltpu.VMEM((k, tn), BF16)],
        compiler_params=_params("parallel", "arbitrary"),
        name="panel_matmul",
    )(*args)


def _matmul_res_body(a_ref, w_ref, r_ref, o_ref, acc_ref):
    kk = pl.program_id(2)

    @pl.when(kk == 0)
    def _():
        acc_ref[...] = jnp.zeros_like(acc_ref)

    acc_ref[...] += jnp.dot(a_ref[...], w_ref[...].astype(BF16), preferred_element_type=F32)

    @pl.when(kk == pl.num_programs(2) - 1)
    def _():
        o_ref[...] = r_ref[...] + acc_ref[...]


def _matmul_res(a, w, layer, res, *, tm=1024, tn=1024, tk=2048):
    m, k = a.shape
    n = w.shape[2]
    tm, tn, tk = _tile(m, tm), _tile(n, tn, V7X_LANES), _tile(k, tk, V7X_LANES)
    return pl.pallas_call(
        _matmul_res_body,
        grid=(m // tm, n // tn, k // tk),
        in_specs=[
            pl.BlockSpec((tm, tk), lambda i, j, kk: (i, kk)),
            pl.BlockSpec((None, tk, tn), lambda i, j, kk: (layer, kk, j)),
            pl.BlockSpec((tm, tn), lambda i, j, kk: (i, j)),
        ],
        out_specs=pl.BlockSpec((tm, tn), lambda i, j, kk: (i, j)),
        out_shape=jax.ShapeDtypeStruct((m, n), F32),
        scratch_shapes=[pltpu.VMEM((tm, tn), F32)],
        compiler_params=_params("parallel", "parallel", "arbitrary"),
        name="matmul_res",
    )(a, w, res)


def _norm_matmul_body(x_ref, g_ref, w_ref, o_ref):
    h = _rms_rows(x_ref[...], g_ref[...]).astype(BF16)
    o_ref[...] = jnp.dot(h, w_ref[...], preferred_element_type=F32)


def _norm_matmul(x, g, w, *, tm=512):
    m, k = x.shape
    n = w.shape[1]
    tm = _tile(m, tm)
    return pl.pallas_call(
        _norm_matmul_body,
        grid=(m // tm,),
        in_specs=[
            pl.BlockSpec((tm, k), lambda i: (i, 0)),
            pl.BlockSpec((1, k), lambda i: (0, 0)),
            pl.BlockSpec((k, n), lambda i: (0, 0)),
        ],
        out_specs=pl.BlockSpec((tm, n), lambda i: (i, 0)),
        out_shape=jax.ShapeDtypeStruct((m, n), F32),
        compiler_params=_params("parallel"),
        name="norm_matmul",
    )(x, g.reshape(1, k), w)


ATTN_NORM_ROWS = 256
ATTN_BLOCKS_PER_ITER = 16


def _attn_body(q_ref, k_ref, v_ref, qg_ref, kg_ref, o_ref, qn_ref, kn_ref, vn_ref, acc_ref, m_ref, l_ref, bias_ref,
               *, dilations, base, blk, scale):
    seq, hd = q_ref.shape
    qg, kg = qg_ref[...], kg_ref[...]
    cls = seq // base
    rc = _tile(cls, ATTN_NORM_ROWS)
    qscale = scale * LOG2_E

    for rb in range(base):
        def norm_chunk(c, carry, rb=rb):
            src = pl.ds(c * (rc * base) + rb, rc, stride=base)
            dst = pl.ds(pl.multiple_of(rb * cls + c * rc, rc), rc)
            qn_ref[dst, :] = _rms_rows(q_ref[src, :], qg) * qscale
            kn_ref[dst, :] = _rms_rows(k_ref[src, :], kg)
            vn_ref[dst, :] = v_ref[src, :]
            return carry

        lax.fori_loop(0, cls // rc, norm_chunk, 0)

    qi = lax.broadcasted_iota(jnp.int32, (blk, 2 * blk), 0)
    kj = lax.broadcasted_iota(jnp.int32, (blk, 2 * blk), 1)
    qseg, kseg = blk // base, 2 * blk // base
    qt = (qi % qseg) * base + qi // qseg
    kt = (kj % kseg) * base + kj // kseg
    for e, (rel, off) in enumerate(((qi - kj, blk), (qi - kj, 0), (qt - kt, blk), (qt - kt, 0))):
        dist = rel + off
        bias_ref[e] = jnp.where((dist >= 0) & (dist <= blk), 0.0, NEG_INF).astype(F32)

    def load(ref, segs):
        parts = [ref[sl, :] for sl, _, _ in segs]
        return parts[0] if len(parts) == 1 else jnp.concatenate(parts, axis=0)

    def store(ref, segs, val):
        for sl, lo, hi in segs:
            ref[sl, :] = val[lo:hi]

    for pi, d in enumerate(dilations):
        nb = seq // (d * blk)
        first = pi == 0

        def views(idx, d=d, nb=nb):
            r, n = idx // nb, idx % nb
            kb = jnp.maximum(n - 1, 0)
            entry = 1 - (n - kb)
            if d % base == 0:
                e = d // base
                origin = (r % base) * cls + r // base
                if e == 1:
                    q_sl = pl.ds(pl.multiple_of(origin + n * blk, blk), blk)
                    k_sl = pl.ds(pl.multiple_of(origin + kb * blk, blk), 2 * blk)
                else:
                    q_sl = pl.ds(origin + n * (blk * e), blk, stride=e)
                    k_sl = pl.ds(origin + kb * (blk * e), 2 * blk, stride=e)
                return [(q_sl, 0, blk)], [(k_sl, 0, 2 * blk)], entry
            q_segs = [(pl.ds(pl.multiple_of(rb * cls + n * qseg, qseg), qseg), rb * qseg, (rb + 1) * qseg)
                      for rb in range(base)]
            k_segs = [(pl.ds(pl.multiple_of(rb * cls + kb * qseg, qseg), kseg), rb * kseg, (rb + 1) * kseg)
                      for rb in range(base)]
            return q_segs, k_segs, entry + 2

        def one_block(idx, views=views, first=first):
            q_segs, k_segs, entry = views(idx)
            q = load(qn_ref, q_segs).astype(BF16)
            kk = load(kn_ref, k_segs).astype(BF16)
            vv = load(vn_ref, k_segs).astype(BF16)
            s = lax.dot_general(q, kk, (((1,), (1,)), ((), ())), preferred_element_type=F32)
            s = s + bias_ref[entry]
            m_b = jnp.max(s, axis=-1, keepdims=True)
            p = jnp.exp2(s - m_b)
            l_b = jnp.sum(p, axis=-1, keepdims=True)
            o_b = jnp.dot(p.astype(BF16), vv, preferred_element_type=F32)
            if first:
                store(m_ref, q_segs, jnp.broadcast_to(m_b, (blk, hd)))
                store(l_ref, q_segs, jnp.broadcast_to(l_b, (blk, hd)))
                store(acc_ref, q_segs, o_b)
            else:
                m_o = load(m_ref, q_segs)
                m_n = jnp.maximum(m_o, m_b)
                alpha = jnp.exp2(m_o - m_n)
                beta = jnp.exp2(m_b - m_n)
                store(m_ref, q_segs, m_n)
                store(l_ref, q_segs, alpha * load(l_ref, q_segs) + beta * l_b)
                store(acc_ref, q_segs, alpha * load(acc_ref, q_segs) + beta * o_b)

        n_blocks = seq // blk
        per_iter = ATTN_BLOCKS_PER_ITER if n_blocks % ATTN_BLOCKS_PER_ITER == 0 else 1

        def block_group(g, carry, one_block=one_block, per_iter=per_iter):
            for u in range(per_iter):
                one_block(g * per_iter + u)
            return carry

        lax.fori_loop(0, n_blocks // per_iter, block_group, 0)

    for rb in range(base):
        def out_chunk(c, carry, rb=rb):
            src = pl.ds(pl.multiple_of(rb * cls + c * rc, rc), rc)
            o_ref[pl.ds(c * (rc * base) + rb, rc, stride=base), :] = acc_ref[src, :] / l_ref[src, :]
            return carry

        lax.fori_loop(0, cls // rc, out_chunk, 0)


def _dilated_attention(proj, q_g, k_g, *, bsz, seq, n_heads, hd, k_col, v_col):
    assert hd == V7X_LANES
    blk = DILATION_PATTERNS[0][0] // DILATION_PATTERNS[0][1]
    for window, d in DILATION_PATTERNS:
        assert window == blk * d and seq % (d * blk) == 0 and seq // (d * blk) >= 2
    dilations = tuple(sorted((d for _, d in DILATION_PATTERNS), reverse=True))
    base = dilations[len(dilations) // 2]
    assert all(d == 1 or d % base == 0 for d in dilations) and blk % (base * V7X_SUBLANES) == 0
    head = lambda off: pl.BlockSpec((seq, hd), lambda b, h: (b, off + h))
    gain = pl.BlockSpec((1, hd), lambda b, h: (0, 0))
    return pl.pallas_call(
        functools.partial(_attn_body, dilations=dilations, base=base, blk=blk, scale=hd ** -0.5),
        grid=(bsz, n_heads),
        in_specs=[head(0), head(k_col), head(v_col), gain, gain],
        out_specs=head(0),
        out_shape=jax.ShapeDtypeStruct((bsz * seq, n_heads * hd), F32),
        scratch_shapes=[pltpu.VMEM((seq, hd), F32)] * 6 + [pltpu.VMEM((4, blk, 2 * blk), F32)],
        compiler_params=_params("parallel", "parallel"),
        name="dilated_attn",
    )(proj, proj, proj, q_g.reshape(1, hd), k_g.reshape(1, hd))


CONV_HALO_ROWS = 32
CONV_ROW_CHUNK = 64
CONV_COL_CHUNK = 256


def _conv_body(a_ref, g_ref, ah_ref, gh_ref, dw_ref, db_ref, lg_ref, lb_ref, mg_ref, o_ref, hs, cbuf, *, width):
    ts, cw = a_ref.shape
    halo = ah_ref.shape[0]
    sub = hs.shape[0]
    i = pl.program_id(1)
    hs[0, halo:halo + ts, :] = a_ref[...] * jax.nn.sigmoid(g_ref[...])

    @pl.when(i > 0)
    def _():
        hs[0, 0:halo, :] = ah_ref[...] * jax.nn.sigmoid(gh_ref[...])

    @pl.when(i == 0)
    def _():
        hs[0, 0:halo, :] = jnp.zeros((halo, cw), F32)

    off = halo - (width - 1)
    rows_b = halo + ts - sub
    for b in range(1, sub):
        hs[b, 0:rows_b, :] = hs[0, b:b + rows_b, :]

    rc, cc = min(CONV_ROW_CHUNK, ts), min(CONV_COL_CHUNK, cw)
    for r0 in range(0, ts, rc):
        for c0 in range(0, cw, cc):
            acc = jnp.zeros((rc, cc), F32)
            for j in range(width):
                a8, b = (off + j) // sub * sub, (off + j) % sub
                acc += hs[b, r0 + a8:r0 + a8 + rc, c0:c0 + cc] * dw_ref[j:j + 1, c0:c0 + cc]
            cbuf[r0:r0 + rc, c0:c0 + cc] = acc + db_ref[:, c0:c0 + cc]

    for r0 in range(0, ts, rc):
        y = cbuf[r0:r0 + rc, :]
        yc = y - jnp.mean(y, axis=-1, keepdims=True)
        yn = yc * lax.rsqrt(jnp.mean(yc * yc, axis=-1, keepdims=True) + EPS) * lg_ref[...] + lb_ref[...]
        z = jax.nn.silu(yn)
        o_ref[r0:r0 + rc, :] = _rms_rows(z, mg_ref[...]).astype(o_ref.dtype)


def _conformer_conv(proj, dw, db, ln_g, ln_b, mix_g, *, bsz, seq, a_col, g_col, ts=256):
    n_cols = proj.shape[1]
    width, cw = dw.shape
    assert width - 1 <= CONV_HALO_ROWS
    ts = _tile(seq, ts)
    assert ts % CONV_HALO_ROWS == 0
    hpt = ts // CONV_HALO_ROWS
    view = proj.reshape(bsz, seq, n_cols)

    def main(col):
        return pl.BlockSpec((None, ts, cw), lambda b, i: (b, i, col))

    def halo(col):
        return pl.BlockSpec((None, CONV_HALO_ROWS, cw), lambda b, i: (b, jnp.maximum(i * hpt - 1, 0), col))

    vec = pl.BlockSpec((1, cw), lambda b, i: (0, 0))
    out = pl.pallas_call(
        functools.partial(_conv_body, width=width),
        grid=(bsz, seq // ts),
        in_specs=[main(a_col), main(g_col), halo(a_col), halo(g_col),
                  pl.BlockSpec((width, cw), lambda b, i: (0, 0)), vec, vec, vec, vec],
        out_specs=pl.BlockSpec((None, ts, cw), lambda b, i: (b, i, 0)),
        out_shape=jax.ShapeDtypeStruct((bsz, seq, cw), BF16),
        scratch_shapes=[pltpu.VMEM((V7X_SUBLANES, CONV_HALO_ROWS + ts, cw), F32), pltpu.VMEM((ts, cw), F32)],
        compiler_params=_params("parallel", "parallel"),
        name="conformer_conv",
    )(view, view, view, view, dw, db.reshape(1, cw), ln_g.reshape(1, cw), ln_b.reshape(1, cw), mix_g.reshape(1, cw))
    return out.reshape(bsz * seq, cw)


def _ssm_chunk_body(u_ref, d_ref, ks_ref, ko_ref, lt_ref, y_ref, kin_ref, ul_ref, inj_ref, x0_ref, *, chunk):
    seq, lanes = u_ref.shape
    nc = seq // chunk
    half = lt_ref.shape[1] // 2

    @pl.when(pl.program_id(1) == 0)
    def _():
        zero = jnp.zeros((lanes, lanes), kin_ref.dtype)
        for s in range(chunk):
            for t in range(chunk):
                kin_ref[s * lanes:(s + 1) * lanes, t * lanes:(t + 1) * lanes] = d_ref[t - s] if t >= s else zero

    for s in range(chunk):
        ul_ref[:, s * lanes:(s + 1) * lanes] = u_ref[pl.ds(s, nc, stride=chunk), :].astype(ul_ref.dtype)
    u = ul_ref[...]
    inj_ref[...] = jnp.dot(u, ks_ref[...], preferred_element_type=F32)
    lr, li = lt_ref[:, :half], lt_ref[:, half:]

    def step(c, carry):
        xr, xi = carry
        x0_ref[pl.ds(c, 1), :half] = xr
        x0_ref[pl.ds(c, 1), half:] = xi
        row = inj_ref[pl.ds(c, 1), :]
        return lr * xr - li * xi + row[:, :half], lr * xi + li * xr + row[:, half:]

    zero = jnp.zeros((1, half), F32)
    lax.fori_loop(0, nc, step, (zero, zero))
    y = jnp.dot(u, kin_ref[...], preferred_element_type=F32)
    y += jnp.dot(x0_ref[...].astype(BF16), ko_ref[...], preferred_element_type=F32)
    for t in range(chunk):
        y_ref[pl.ds(t, nc, stride=chunk), :] = y[:, t * lanes:(t + 1) * lanes]


def _ssm_chunks(proj, lag_blocks, k_state, k_out, lam_t, *, bsz, seq, u_col, chunk):
    nj, _, lanes, _ = lag_blocks.shape
    tl = chunk * lanes
    st = k_state.shape[2]
    nc = seq // chunk
    mat = lambda a, b: pl.BlockSpec((None, a, b), lambda j, bb: (j, 0, 0))
    return pl.pallas_call(
        functools.partial(_ssm_chunk_body, chunk=chunk),
        grid=(nj, bsz),
        in_specs=[
            pl.BlockSpec((seq, lanes), lambda j, bb: (bb, u_col + j)),
            pl.BlockSpec((None, chunk, lanes, lanes), lambda j, bb: (j, 0, 0, 0)),
            mat(tl, st), mat(st, tl), mat(1, st),
        ],
        out_specs=pl.BlockSpec((seq, lanes), lambda j, bb: (bb, j)),
        out_shape=jax.ShapeDtypeStruct((bsz * seq, nj * lanes), F32),
        scratch_shapes=[pltpu.VMEM((tl, tl), BF16), pltpu.VMEM((nc, tl), BF16),
                        pltpu.VMEM((nc, st), F32), pltpu.VMEM((nc, st), F32)],
        compiler_params=_params("parallel", "arbitrary"),
        name="ssm_chunks",
    )(proj, lag_blocks, k_state, k_out, lam_t)


def _ssm_post_body(y_ref, u_ref, d_ref, wg_ref, bg_ref, mg_ref, o_ref):
    z = jax.nn.gelu(y_ref[...] + d_ref[...] * u_ref[...])
    gate = jnp.dot(z.astype(BF16), wg_ref[...], preferred_element_type=F32) + bg_ref[...]
    out = z * jax.nn.sigmoid(gate)
    o_ref[...] = _rms_rows(out, mg_ref[...]).astype(o_ref.dtype)


def _ssm_post(y, proj, d_skip, w_glu, b_glu, mix_g, *, u_col, tm=512):
    m, sw = y.shape
    tm = _tile(m, tm)
    vec = pl.BlockSpec((1, sw), lambda i: (0, 0))
    return pl.pallas_call(
        _ssm_post_body,
        grid=(m // tm,),
        in_specs=[pl.BlockSpec((tm, sw), lambda i: (i, 0)), pl.BlockSpec((tm, sw), lambda i: (i, u_col)),
                  vec, pl.BlockSpec((sw, sw), lambda i: (0, 0)), vec, vec],
        out_specs=pl.BlockSpec((tm, sw), lambda i: (i, 0)),
        out_shape=jax.ShapeDtypeStruct((m, sw), BF16),
        compiler_params=_params("parallel"),
        name="ssm_post",
    )(y, proj, d_skip.reshape(1, sw), w_glu, b_glu.reshape(1, sw), mix_g.reshape(1, sw))


def _ssm_chunk_matrices(a_re, a_im, b_re, b_im, c_re, c_im, log_step, chunk):
    n_groups, n_state = a_re.shape
    n_ch = b_re.shape[-1]
    gpb = V7X_LANES // n_ch
    assert V7X_LANES % n_ch == 0 and n_groups % gpb == 0
    nj = n_groups // gpb
    hi = lax.Precision.HIGHEST

    step = jnp.exp(log_step)[:, None]
    mag, ang = jnp.exp(a_re * step), a_im * step
    lr, li = mag * jnp.cos(ang), mag * jnp.sin(ang)
    den = a_re * a_re + a_im * a_im
    fr = ((lr - 1.0) * a_re + li * a_im) / den
    fi = (li * a_re - (lr - 1.0) * a_im) / den
    bbr = fr[..., None] * b_re - fi[..., None] * b_im
    bbi = fr[..., None] * b_im + fi[..., None] * b_re

    prs, pis = [jnp.ones_like(lr)], [jnp.zeros_like(lr)]
    for _ in range(chunk):
        prs.append(prs[-1] * lr - pis[-1] * li)
        pis.append(prs[-2] * li + pis[-1] * lr)
    pr, pi = jnp.stack(prs), jnp.stack(pis)

    wr = pr[:chunk, :, :, None] * bbr - pi[:chunk, :, :, None] * bbi
    wi = pr[:chunk, :, :, None] * bbi + pi[:chunk, :, :, None] * bbr
    kt = (jnp.einsum("ghp,tgpk->gtkh", c_re, wr, precision=hi)
          - jnp.einsum("ghp,tgpk->gtkh", c_im, wi, precision=hi))

    eye = jnp.eye(gpb, dtype=F32)
    tl = chunk * V7X_LANES
    st = 2 * gpb * n_state

    ktj = kt.reshape(nj, gpb, chunk, n_ch, n_ch).transpose(0, 2, 1, 3, 4)
    lag_blocks = (ktj[:, :, :, :, None, :] * eye[None, None, :, None, :, None]).astype(BF16)
    lag_blocks = lag_blocks.reshape(nj, chunk, V7X_LANES, V7X_LANES)

    def by_block(a):
        return a.reshape(a.shape[0], nj, gpb, n_state).transpose(1, 0, 2, 3)

    qr, qi = by_block(jnp.stack(prs[chunk - 1::-1])), by_block(jnp.stack(pis[chunk - 1::-1]))
    b4r = bbr.reshape(nj, 1, gpb, n_state, n_ch).transpose(0, 1, 2, 4, 3)
    b4i = bbi.reshape(nj, 1, gpb, n_state, n_ch).transpose(0, 1, 2, 4, 3)
    qr, qi = qr[:, :, :, None, :], qi[:, :, :, None, :]
    w2 = jnp.stack([qr * b4r - qi * b4i, qr * b4i + qi * b4r], axis=4)
    k_state = (w2[:, :, :, :, :, None, :] * eye[None, None, :, None, None, :, None]).astype(BF16).reshape(nj, tl, st)

    er = pr[1:].reshape(chunk, nj, gpb, n_state).transpose(1, 2, 3, 0)[..., None]
    ei = pi[1:].reshape(chunk, nj, gpb, n_state).transpose(1, 2, 3, 0)[..., None]
    c4r = c_re.reshape(nj, gpb, n_ch, n_state).transpose(0, 1, 3, 2)[:, :, :, None, :]
    c4i = c_im.reshape(nj, gpb, n_ch, n_state).transpose(0, 1, 3, 2)[:, :, :, None, :]
    cl2 = jnp.stack([c4r * er - c4i * ei, -(c4r * ei + c4i * er)], axis=1)
    k_out = (cl2[:, :, :, :, :, None, :] * eye[None, None, :, None, None, :, None]).astype(BF16).reshape(nj, st, tl)

    lam_t = jnp.stack([pr[chunk], pi[chunk]]).reshape(2, nj, gpb * n_state).transpose(1, 0, 2).reshape(nj, 1, st)
    return lag_blocks, k_state, k_out, lam_t


def _cross_body(x_ref, g_ref, wq_ref, kv_ref, qg_ref, kg_ref, wo_ref, ng_ref, o_ref, hn_ref, *, n_heads, hd):
    x = x_ref[...]
    h = _rms_rows(x, g_ref[...]).astype(BF16)
    q = jnp.dot(h, wq_ref[...], preferred_element_type=F32)
    mw = n_heads * hd
    scale = hd ** -0.5
    heads = []
    for hh in range(n_heads):
        cs = slice(hh * hd, (hh + 1) * hd)
        qn = (_rms_rows(q[:, cs], qg_ref[...]) * scale).astype(BF16)
        kn = _rms_rows(kv_ref[:, cs], kg_ref[...]).astype(BF16)
        v = kv_ref[:, mw + hh * hd:mw + (hh + 1) * hd].astype(BF16)
        s = lax.dot_general(qn, kn, (((1,), (1,)), ((), ())), preferred_element_type=F32)
        e = jnp.exp(s - jnp.max(s, axis=-1, keepdims=True))
        p = e / jnp.sum(e, axis=-1, keepdims=True)
        heads.append(jnp.dot(p.astype(BF16), v, preferred_element_type=F32))
    o = jnp.concatenate(heads, axis=-1).astype(BF16)
    y = x + jnp.dot(o, wo_ref[...], preferred_element_type=F32)
    o_ref[...] = y
    hn_ref[...] = _rms_rows(y, ng_ref[...]).astype(hn_ref.dtype)


def _cross_attention(x, g, w_cq, kv, cq_g, ck_g, w_co, next_g, *, bsz, seq, mem_len, tm=256):
    m, d = x.shape
    mw = w_cq.shape[1]
    hd = cq_g.shape[-1]
    n_heads = mw // hd
    tm = _tile(seq, tm)
    spb = seq // tm
    const = lambda a, b: pl.BlockSpec((a, b), lambda bb, i: (0, 0))
    rows = pl.BlockSpec((tm, d), lambda bb, i: (bb * spb + i, 0))
    return pl.pallas_call(
        functools.partial(_cross_body, n_heads=n_heads, hd=hd),
        grid=(bsz, spb),
        in_specs=[
            rows, const(1, d), const(d, mw),
            pl.BlockSpec((mem_len, 2 * mw), lambda bb, i: (bb, 0)),
            const(1, hd), const(1, hd), const(mw, d), const(1, d),
        ],
        out_specs=[rows, rows],
        out_shape=[jax.ShapeDtypeStruct((m, d), F32), jax.ShapeDtypeStruct((m, d), BF16)],
        compiler_params=_params("parallel", "parallel"),
        name="cross_attn",
    )(x, g.reshape(1, d), w_cq, kv, cq_g.reshape(1, hd), ck_g.reshape(1, hd), w_co, next_g.reshape(1, d))


def kernel(x, mem, norm_mix, w_in, q_norm, k_norm, conv_dw, conv_b, conv_ln_g, conv_ln_b, ssm_a_re, ssm_a_im, ssm_b_re, ssm_b_im, ssm_c_re, ssm_c_im, ssm_d, ssm_log_step, ssm_w_glu, ssm_b_glu, mix_out_norm, w_out, norm_cross, norm_mem, w_cq, w_ckv, cq_norm, ck_norm, w_co, norm_mlp, w_up, w_down):
    bsz, seq, d_model = x.shape
    mem_len = mem.shape[1]
    depth = w_in.shape[0]
    hd = q_norm.shape[-1]
    conv_w = conv_b.shape[-1]
    ssm_w = ssm_d.shape[-1]
    attn_w = w_out.shape[1] - conv_w - ssm_w
    n_heads = attn_w // hd
    conv_off = 3 * attn_w
    ssm_off = conv_off + 2 * conv_w
    assert ssm_off + ssm_w == w_in.shape[-1]
    assert conv_off % conv_w == 0 and ssm_off % ssm_w == 0 and ssm_w % V7X_LANES == 0

    xf = x.reshape(bsz * seq, d_model)
    memf = mem.reshape(bsz * mem_len, d_model)
    for l in range(depth):
        mix_g = mix_out_norm[l]

        proj = _panel_matmul([_rms_norm_bf16(xf, norm_mix[l])], w_in, l)

        attn = _dilated_attention(proj, q_norm[l], k_norm[l], bsz=bsz, seq=seq, n_heads=n_heads, hd=hd,
                                  k_col=n_heads, v_col=2 * n_heads)
        attn_n = _rms_norm_bf16(attn, mix_g[:attn_w])

        conv_n = _conformer_conv(proj, conv_dw[l].reshape(conv_dw.shape[1], conv_w), conv_b[l], conv_ln_g[l],
                                 conv_ln_b[l], mix_g[attn_w:attn_w + conv_w], bsz=bsz, seq=seq,
                                 a_col=conv_off // conv_w, g_col=conv_off // conv_w + 1)

        lag_blocks, k_state, k_out, lam_t = _ssm_chunk_matrices(
            ssm_a_re[l], ssm_a_im[l], ssm_b_re[l], ssm_b_im[l], ssm_c_re[l], ssm_c_im[l], ssm_log_step[l], SSM_CHUNK)
        y = _ssm_chunks(proj, lag_blocks, k_state, k_out, lam_t, bsz=bsz, seq=seq,
                        u_col=ssm_off // V7X_LANES, chunk=SSM_CHUNK)
        ssm_n = _ssm_post(y, proj, ssm_d[l], ssm_w_glu[l].astype(BF16), ssm_b_glu[l], mix_g[attn_w + conv_w:],
                          u_col=ssm_off // ssm_w)

        xf = _panel_matmul([attn_n, conv_n, ssm_n], w_out, l, res=xf)

        kv = _norm_matmul(memf, norm_mem[l], w_ckv[l].astype(BF16))
        xf, h_mlp = _cross_attention(xf, norm_cross[l], w_cq[l].astype(BF16), kv, cq_norm[l], ck_norm[l],
                                     w_co[l].astype(BF16), norm_mlp[l], bsz=bsz, seq=seq, mem_len=mem_len)

        hid = _panel_matmul([h_mlp], w_up, l, act="relu2", out_dtype=BF16)
        xf = _matmul_res(hid, w_down, l, xf)
    return xf.reshape(bsz, seq, d_model)
```

```python
import functools

import jax
import jax.numpy as jnp
from jax import lax
from jax.experimental import pallas as pl
from jax.experimental.pallas import tpu as pltpu

F32 = jnp.float32
BF16 = jnp.bfloat16

EPS = 1e-6
NEG_INF = -1e30
LOG2_E = 1.4426950408889634
DILATION_PATTERNS = ((128, 1), (512, 4), (2048, 16))

V7X_LANES = 128
V7X_SUBLANES = 8
V7X_VMEM_LIMIT_BYTES = 60000 * 1024
SSM_CHUNK = 8


def _tile(n, pref, align=8):
    if n <= pref:
        return n
    for t in range(pref - pref % align, 0, -align):
        if n % t == 0:
            return t
    raise ValueError((n, pref, align))


def _params(*semantics):
    return pltpu.CompilerParams(dimension_semantics=semantics, vmem_limit_bytes=V7X_VMEM_LIMIT_BYTES)


def _rms_rows(x, g):
    return x * lax.rsqrt(jnp.mean(x * x, axis=-1, keepdims=True) + EPS) * g


def _norm_body(x_ref, g_ref, o_ref):
    o_ref[...] = _rms_rows(x_ref[...], g_ref[...]).astype(o_ref.dtype)


def _rms_norm_bf16(x, g, *, tm=256):
    m, k = x.shape
    tm = _tile(m, tm)
    return pl.pallas_call(
        _norm_body,
        grid=(m // tm,),
        in_specs=[pl.BlockSpec((tm, k), lambda i: (i, 0)), pl.BlockSpec((1, k), lambda i: (0, 0))],
        out_specs=pl.BlockSpec((tm, k), lambda i: (i, 0)),
        out_shape=jax.ShapeDtypeStruct((m, k), BF16),
        compiler_params=_params("parallel"),
        name="rms_norm",
    )(x, g.reshape(1, k))


WEIGHT_CAST_ROWS = 512


def _cast_panel(w_ref, wb_ref):
    rows = wb_ref.shape[0]
    rc = _tile(rows, WEIGHT_CAST_ROWS)

    def chunk(c, carry):
        sl = pl.ds(pl.multiple_of(c * rc, rc), rc)
        wb_ref[sl, :] = w_ref[sl, :].astype(wb_ref.dtype)
        return carry

    lax.fori_loop(0, rows // rc, chunk, 0)


def _panel_matmul_body(*refs, n_lhs, act, has_res):
    a_refs = refs[:n_lhs]
    w_ref = refs[n_lhs]
    r_ref = refs[n_lhs + 1] if has_res else None
    o_ref, wb_ref = refs[-2], refs[-1]

    @pl.when(pl.program_id(1) == 0)
    def _():
        _cast_panel(w_ref, wb_ref)

    acc = None
    k0 = 0
    for a_ref in a_refs:
        kk = a_ref.shape[1]
        part = jnp.dot(a_ref[...], wb_ref[k0:k0 + kk, :], preferred_element_type=F32)
        acc = part if acc is None else acc + part
        k0 += kk
    if act == "relu2":
        acc = jnp.square(jnp.maximum(acc, 0.0))
    if has_res:
        acc = r_ref[...] + acc
    o_ref[...] = acc.astype(o_ref.dtype)


def _panel_matmul(lhs, w, layer, *, res=None, act=None, out_dtype=F32, tm=1024, tn=512):
    m = lhs[0].shape[0]
    k, n = w.shape[1], w.shape[2]
    assert sum(a.shape[1] for a in lhs) == k
    tm, tn = _tile(m, tm), _tile(n, tn, V7X_LANES)
    in_specs = [pl.BlockSpec((tm, a.shape[1]), lambda j, i: (i, 0)) for a in lhs]
    in_specs.append(pl.BlockSpec((None, k, tn), lambda j, i: (layer, 0, j)))
    args = list(lhs) + [w]
    if res is not None:
        in_specs.append(pl.BlockSpec((tm, tn), lambda j, i: (i, j)))
        args.append(res)
    return pl.pallas_call(
        functools.partial(_panel_matmul_body, n_lhs=len(lhs), act=act, has_res=res is not None),
        grid=(n // tn, m // tm),
        in_specs=in_specs,
        out_specs=pl.BlockSpec((tm, tn), lambda j, i: (i, j)),
        out_shape=jax.ShapeDtypeStruct((m, n), out_dtype),
        scratch_shapes=[pltpu.VMEM((k, tn), BF16)],
        compiler_params=_params("parallel", "arbitrary"),
        name="panel_matmul",
    )(*args)


def _matmul_res_body(a_ref, w_ref, r_ref, o_ref, acc_ref):
    kk = pl.program_id(2)

    @pl.when(kk == 0)
    def _():
        acc_ref[...] = jnp.zeros_like(acc_ref)

    acc_ref[...] += jnp.dot(a_ref[...], w_ref[...].astype(BF16), preferred_element_type=F32)

    @pl.when(kk == pl.num_programs(2) - 1)
    def _():
        o_ref[...] = r_ref[...] + acc_ref[...]


def _matmul_res(a, w, layer, res, *, tm=1024, tn=1024, tk=2048):
    m, k = a.shape
    n = w.shape[2]
    tm, tn, tk = _tile(m, tm), _tile(n, tn, V7X_LANES), _tile(k, tk, V7X_LANES)
    return pl.pallas_call(
        _matmul_res_body,
        grid=(m // tm, n // tn, k // tk),
        in_specs=[
            pl.BlockSpec((tm, tk), lambda i, j, kk: (i, kk)),
            pl.BlockSpec((None, tk, tn), lambda i, j, kk: (layer, kk, j)),
            pl.BlockSpec((tm, tn), lambda i, j, kk: (i, j)),
        ],
        out_specs=pl.BlockSpec((tm, tn), lambda i, j, kk: (i, j)),
        out_shape=jax.ShapeDtypeStruct((m, n), F32),
        scratch_shapes=[pltpu.VMEM((tm, tn), F32)],
        compiler_params=_params("parallel", "parallel", "arbitrary"),
        name="matmul_res",
    )(a, w, res)


def _norm_matmul_body(x_ref, g_ref, w_ref, o_ref):
    h = _rms_rows(x_ref[...], g_ref[...]).astype(BF16)
    o_ref[...] = jnp.dot(h, w_ref[...], preferred_element_type=F32)


def _norm_matmul(x, g, w, *, tm=512):
    m, k = x.shape
    n = w.shape[1]
    tm = _tile(m, tm)
    return pl.pallas_call(
        _norm_matmul_body,
        grid=(m // tm,),
        in_specs=[
            pl.BlockSpec((tm, k), lambda i: (i, 0)),
            pl.BlockSpec((1, k), lambda i: (0, 0)),
            pl.BlockSpec((k, n), lambda i: (0, 0)),
        ],
        out_specs=pl.BlockSpec((tm, n), lambda i: (i, 0)),
        out_shape=jax.ShapeDtypeStruct((m, n), F32),
        compiler_params=_params("parallel"),
        name="norm_matmul",
    )(x, g.reshape(1, k), w)


ATTN_NORM_ROWS = 256
ATTN_BLOCKS_PER_ITER = 16


def _attn_body(q_ref, k_ref, v_ref, qg_ref, kg_ref, o_ref, qn_ref, kn_ref, vn_ref, acc_ref, m_ref, l_ref, bias_ref,
               *, dilations, base, blk, scale):
    seq, hd = q_ref.shape
    qg, kg = qg_ref[...], kg_ref[...]
    cls = seq // base
    rc = _tile(cls, ATTN_NORM_ROWS)
    qscale = scale * LOG2_E

    def norm_chunk(c, carry):
        for rb in range(base):
            src = pl.ds(c * (rc * base) + rb, rc, stride=base)
            dst = pl.ds(pl.multiple_of(rb * cls + c * rc, rc), rc)
            qn_ref[dst, :] = _rms_rows(q_ref[src, :], qg) * qscale
            kn_ref[dst, :] = _rms_rows(k_ref[src, :], kg)
            vn_ref[dst, :] = v_ref[src, :]
        return carry

    lax.fori_loop(0, cls // rc, norm_chunk, 0)

    qi = lax.broadcasted_iota(jnp.int32, (blk, 2 * blk), 0)
    kj = lax.broadcasted_iota(jnp.int32, (blk, 2 * blk), 1)
    qseg, kseg = blk // base, 2 * blk // base
    qt = (qi % qseg) * base + qi // qseg
    kt = (kj % kseg) * base + kj // kseg
    for e, (rel, off) in enumerate(((qi - kj, blk), (qi - kj, 0), (qt - kt, blk), (qt - kt, 0))):
        dist = rel + off
        bias_ref[e] = jnp.where((dist >= 0) & (dist <= blk), 0.0, NEG_INF).astype(F32)

    def load(ref, segs):
        parts = [ref[sl, :] for sl, _, _ in segs]
        return parts[0] if len(parts) == 1 else jnp.concatenate(parts, axis=0)

    def store(ref, segs, val):
        for sl, lo, hi in segs:
            ref[sl, :] = val[lo:hi]

    for pi, d in enumerate(dilations):
        nb = seq // (d * blk)
        first = pi == 0

        def views(idx, d=d, nb=nb):
            r, n = idx // nb, idx % nb
            kb = jnp.maximum(n - 1, 0)
            entry = 1 - (n - kb)
            if d % base == 0:
                e = d // base
                origin = (r % base) * cls + r // base
                if e == 1:
                    q_sl = pl.ds(pl.multiple_of(origin + n * blk, blk), blk)
                    k_sl = pl.ds(pl.multiple_of(origin + kb * blk, blk), 2 * blk)
                else:
                    q_sl = pl.ds(origin + n * (blk * e), blk, stride=e)
                    k_sl = pl.ds(origin + kb * (blk * e), 2 * blk, stride=e)
                return [(q_sl, 0, blk)], [(k_sl, 0, 2 * blk)], entry
            q_segs = [(pl.ds(pl.multiple_of(rb * cls + n * qseg, qseg), qseg), rb * qseg, (rb + 1) * qseg)
                      for rb in range(base)]
            k_segs = [(pl.ds(pl.multiple_of(rb * cls + kb * qseg, qseg), kseg), rb * kseg, (rb + 1) * kseg)
                      for rb in range(base)]
            return q_segs, k_segs, entry + 2

        def one_block(idx, views=views, first=first):
            q_segs, k_segs, entry = views(idx)
            q = load(qn_ref, q_segs).astype(BF16)
            kk = load(kn_ref, k_segs).astype(BF16)
            vv = load(vn_ref, k_segs).astype(BF16)
            s = lax.dot_general(q, kk, (((1,), (1,)), ((), ())), preferred_element_type=F32)
            s = s + bias_ref[entry]
            m_b = jnp.max(s, axis=-1, keepdims=True)
            p = jnp.exp2(s - m_b)
            l_b = jnp.sum(p, axis=-1, keepdims=True)
            o_b = jnp.dot(p.astype(BF16), vv, preferred_element_type=F32)
            if first:
                store(m_ref, q_segs, jnp.broadcast_to(m_b, (blk, hd)))
                store(l_ref, q_segs, jnp.broadcast_to(l_b, (blk, hd)))
                store(acc_ref, q_segs, o_b)
            else:
                m_o = load(m_ref, q_segs)
                m_n = jnp.maximum(m_o, m_b)
                alpha = jnp.exp2(m_o - m_n)
                beta = jnp.exp2(m_b - m_n)
                store(m_ref, q_segs, m_n)
                store(l_ref, q_segs, alpha * load(l_ref, q_segs) + beta * l_b)
                store(acc_ref, q_segs, alpha * load(acc_ref, q_segs) + beta * o_b)

        n_blocks = seq // blk
        per_iter = ATTN_BLOCKS_PER_ITER if n_blocks % ATTN_BLOCKS_PER_ITER == 0 else 1

        def block_group(g, carry, one_block=one_block, per_iter=per_iter):
            for u in range(per_iter):
                one_block(g * per_iter + u)
            return carry

        lax.fori_loop(0, n_blocks // per_iter, block_group, 0)

    def out_chunk(c, carry):
        for rb in range(base):
            src = pl.ds(pl.multiple_of(rb * cls + c * rc, rc), rc)
            o_ref[pl.ds(c * (rc * base) + rb, rc, stride=base), :] = acc_ref[src, :] / l_ref[src, :]
        return carry

    lax.fori_loop(0, cls // rc, out_chunk, 0)


def _dilated_attention(proj, q_g, k_g, *, bsz, seq, n_heads, hd, k_col, v_col):
    assert hd == V7X_LANES
    blk = DILATION_PATTERNS[0][0] // DILATION_PATTERNS[0][1]
    for window, d in DILATION_PATTERNS:
        assert window == blk * d and seq % (d * blk) == 0 and seq // (d * blk) >= 2
    dilations = tuple(sorted((d for _, d in DILATION_PATTERNS), reverse=True))
    base = dilations[len(dilations) // 2]
    assert all(d == 1 or d % base == 0 for d in dilations) and blk % (base * V7X_SUBLANES) == 0
    head = lambda off: pl.BlockSpec((seq, hd), lambda b, h: (b, off + h))
    gain = pl.BlockSpec((1, hd), lambda b, h: (0, 0))
    return pl.pallas_call(
        functools.partial(_attn_body, dilations=dilations, base=base, blk=blk, scale=hd ** -0.5),
        grid=(bsz, n_heads),
        in_specs=[head(0), head(k_col), head(v_col), gain, gain],
        out_specs=head(0),
        out_shape=jax.ShapeDtypeStruct((bsz * seq, n_heads * hd), F32),
        scratch_shapes=[pltpu.VMEM((seq, hd), F32)] * 6 + [pltpu.VMEM((4, blk, 2 * blk), F32)],
        compiler_params=_params("parallel", "parallel"),
        name="dilated_attn",
    )(proj, proj, proj, q_g.reshape(1, hd), k_g.reshape(1, hd))


CONV_HALO_ROWS = 32
CONV_ROW_CHUNK = 64
CONV_COL_CHUNK = 256


def _conv_body(a_ref, g_ref, ah_ref, gh_ref, dw_ref, db_ref, lg_ref, lb_ref, mg_ref, o_ref, hs, cbuf, *, width):
    ts, cw = a_ref.shape
    halo = ah_ref.shape[0]
    sub = hs.shape[0]
    i = pl.program_id(1)
    hs[0, halo:halo + ts, :] = a_ref[...] * jax.nn.sigmoid(g_ref[...])

    @pl.when(i > 0)
    def _():
        hs[0, 0:halo, :] = ah_ref[...] * jax.nn.sigmoid(gh_ref[...])

    @pl.when(i == 0)
    def _():
        hs[0, 0:halo, :] = jnp.zeros((halo, cw), F32)

    off = halo - (width - 1)
    rows_b = halo + ts - sub
    for b in range(1, sub):
        hs[b, 0:rows_b, :] = hs[0, b:b + rows_b, :]

    rc, cc = min(CONV_ROW_CHUNK, ts), min(CONV_COL_CHUNK, cw)
    for r0 in range(0, ts, rc):
        for c0 in range(0, cw, cc):
            acc = jnp.zeros((rc, cc), F32)
            for j in range(width):
                a8, b = (off + j) // sub * sub, (off + j) % sub
                acc += hs[b, r0 + a8:r0 + a8 + rc, c0:c0 + cc] * dw_ref[j:j + 1, c0:c0 + cc]
            cbuf[r0:r0 + rc, c0:c0 + cc] = acc + db_ref[:, c0:c0 + cc]

    for r0 in range(0, ts, rc):
        y = cbuf[r0:r0 + rc, :]
        yc = y - jnp.mean(y, axis=-1, keepdims=True)
        yn = yc * lax.rsqrt(jnp.mean(yc * yc, axis=-1, keepdims=True) + EPS) * lg_ref[...] + lb_ref[...]
        z = jax.nn.silu(yn)
        o_ref[r0:r0 + rc, :] = _rms_rows(z, mg_ref[...]).astype(o_ref.dtype)


def _conformer_conv(proj, dw, db, ln_g, ln_b, mix_g, *, bsz, seq, a_col, g_col, ts=256):
    n_cols = proj.shape[1]
    width, cw = dw.shape
    assert width - 1 <= CONV_HALO_ROWS
    ts = _tile(seq, ts)
    assert ts % CONV_HALO_ROWS == 0
    hpt = ts // CONV_HALO_ROWS
    view = proj.reshape(bsz, seq, n_cols)

    def main(col):
        return pl.BlockSpec((None, ts, cw), lambda b, i: (b, i, col))

    def halo(col):
        return pl.BlockSpec((None, CONV_HALO_ROWS, cw), lambda b, i: (b, jnp.maximum(i * hpt - 1, 0), col))

    vec = pl.BlockSpec((1, cw), lambda b, i: (0, 0))
    out = pl.pallas_call(
        functools.partial(_conv_body, width=width),
        grid=(bsz, seq // ts),
        in_specs=[main(a_col), main(g_col), halo(a_col), halo(g_col),
                  pl.BlockSpec((width, cw), lambda b, i: (0, 0)), vec, vec, vec, vec],
        out_specs=pl.BlockSpec((None, ts, cw), lambda b, i: (b, i, 0)),
        out_shape=jax.ShapeDtypeStruct((bsz, seq, cw), BF16),
        scratch_shapes=[pltpu.VMEM((V7X_SUBLANES, CONV_HALO_ROWS + ts, cw), F32), pltpu.VMEM((ts, cw), F32)],
        compiler_params=_params("parallel", "parallel"),
        name="conformer_conv",
    )(view, view, view, view, dw, db.reshape(1, cw), ln_g.reshape(1, cw), ln_b.reshape(1, cw), mix_g.reshape(1, cw))
    return out.reshape(bsz * seq, cw)


def _ssm_chunk_body(u_ref, d_ref, ks_ref, ko_ref, lt_ref, y_ref, kin_ref, ul_ref, inj_ref, x0_ref, *, chunk):
    seq, lanes = u_ref.shape
    nc = seq // chunk
    half = lt_ref.shape[1] // 2

    @pl.when(pl.program_id(1) == 0)
    def _():
        zero = jnp.zeros((lanes, lanes), kin_ref.dtype)
        for s in range(chunk):
            for t in range(chunk):
                kin_ref[s * lanes:(s + 1) * lanes, t * lanes:(t + 1) * lanes] = d_ref[t - s] if t >= s else zero

    for s in range(chunk):
        ul_ref[:, s * lanes:(s + 1) * lanes] = u_ref[pl.ds(s, nc, stride=chunk), :].astype(ul_ref.dtype)
    u = ul_ref[...]
    inj_ref[...] = jnp.dot(u, ks_ref[...], preferred_element_type=F32)
    lr, li = lt_ref[:, :half], lt_ref[:, half:]

    def step(c, carry):
        xr, xi = carry
        x0_ref[pl.ds(c, 1), :half] = xr
        x0_ref[pl.ds(c, 1), half:] = xi
        row = inj_ref[pl.ds(c, 1), :]
        return lr * xr - li * xi + row[:, :half], lr * xi + li * xr + row[:, half:]

    zero = jnp.zeros((1, half), F32)
    lax.fori_loop(0, nc, step, (zero, zero))
    y = jnp.dot(u, kin_ref[...], preferred_element_type=F32)
    y += jnp.dot(x0_ref[...].astype(BF16), ko_ref[...], preferred_element_type=F32)
    for t in range(chunk):
        y_ref[pl.ds(t, nc, stride=chunk), :] = y[:, t * lanes:(t + 1) * lanes]


def _ssm_chunks(proj, lag_blocks, k_state, k_out, lam_t, *, bsz, seq, u_col, chunk):
    nj, _, lanes, _ = lag_blocks.shape
    tl = chunk * lanes
    st = k_state.shape[2]
    nc = seq // chunk
    mat = lambda a, b: pl.BlockSpec((None, a, b), lambda j, bb: (j, 0, 0))
    return pl.pallas_call(
        functools.partial(_ssm_chunk_body, chunk=chunk),
        grid=(nj, bsz),
        in_specs=[
            pl.BlockSpec((seq, lanes), lambda j, bb: (bb, u_col + j)),
            pl.BlockSpec((None, chunk, lanes, lanes), lambda j, bb: (j, 0, 0, 0)),
            mat(tl, st), mat(st, tl), mat(1, st),
        ],
        out_specs=pl.BlockSpec((seq, lanes), lambda j, bb: (bb, j)),
        out_shape=jax.ShapeDtypeStruct((bsz * seq, nj * lanes), F32),
        scratch_shapes=[pltpu.VMEM((tl, tl), BF16), pltpu.VMEM((nc, tl), BF16),
                        pltpu.VMEM((nc, st), F32), pltpu.VMEM((nc, st), F32)],
        compiler_params=_params("parallel", "arbitrary"),
        name="ssm_chunks",
    )(proj, lag_blocks, k_state, k_out, lam_t)


def _ssm_post_body(y_ref, u_ref, d_ref, wg_ref, bg_ref, mg_ref, o_ref):
    z = jax.nn.gelu(y_ref[...] + d_ref[...] * u_ref[...])
    gate = jnp.dot(z.astype(BF16), wg_ref[...], preferred_element_type=F32) + bg_ref[...]
    out = z * jax.nn.sigmoid(gate)
    o_ref[...] = _rms_rows(out, mg_ref[...]).astype(o_ref.dtype)


def _ssm_post(y, proj, d_skip, w_glu, b_glu, mix_g, *, u_col, tm=512):
    m, sw = y.shape
    tm = _tile(m, tm)
    vec = pl.BlockSpec((1, sw), lambda i: (0, 0))
    return pl.pallas_call(
        _ssm_post_body,
        grid=(m // tm,),
        in_specs=[pl.BlockSpec((tm, sw), lambda i: (i, 0)), pl.BlockSpec((tm, sw), lambda i: (i, u_col)),
                  vec, pl.BlockSpec((sw, sw), lambda i: (0, 0)), vec, vec],
        out_specs=pl.BlockSpec((tm, sw), lambda i: (i, 0)),
        out_shape=jax.ShapeDtypeStruct((m, sw), BF16),
        compiler_params=_params("parallel"),
        name="ssm_post",
    )(y, proj, d_skip.reshape(1, sw), w_glu, b_glu.reshape(1, sw), mix_g.reshape(1, sw))


def _spread_over_groups(compact, *, inner_r, inner_c, gpb):
    _, n_rows, n_c = compact.shape
    n_cols = n_c * gpb
    col = jnp.arange(n_cols)
    src = (col // (gpb * inner_c)) * inner_c + col % inner_c
    spread = (jnp.arange(n_c)[:, None] == src[None, :]).astype(BF16)
    wide = jnp.einsum("jrc,cn->jrn", compact.astype(BF16), spread, preferred_element_type=BF16)
    same = ((jnp.arange(n_rows) // inner_r) % gpb)[:, None] == ((col // inner_c) % gpb)[None, :]
    return jnp.where(same[None], wide, jnp.zeros((), BF16))


def _ssm_chunk_matrices(a_re, a_im, b_re, b_im, c_re, c_im, log_step, chunk):
    n_groups, n_state = a_re.shape
    n_ch = b_re.shape[-1]
    gpb = V7X_LANES // n_ch
    assert V7X_LANES % n_ch == 0 and n_groups % gpb == 0
    nj = n_groups // gpb
    hi = lax.Precision.HIGHEST

    step = jnp.exp(log_step)[:, None]
    mag, ang = jnp.exp(a_re * step), a_im * step
    lr, li = mag * jnp.cos(ang), mag * jnp.sin(ang)
    den = a_re * a_re + a_im * a_im
    fr = ((lr - 1.0) * a_re + li * a_im) / den
    fi = (li * a_re - (lr - 1.0) * a_im) / den
    bbr = fr[..., None] * b_re - fi[..., None] * b_im
    bbi = fr[..., None] * b_im + fi[..., None] * b_re

    prs, pis = [jnp.ones_like(lr)], [jnp.zeros_like(lr)]
    for _ in range(chunk):
        prs.append(prs[-1] * lr - pis[-1] * li)
        pis.append(prs[-2] * li + pis[-1] * lr)
    pr, pi = jnp.stack(prs), jnp.stack(pis)

    wr = pr[:chunk, :, :, None] * bbr - pi[:chunk, :, :, None] * bbi
    wi = pr[:chunk, :, :, None] * bbi + pi[:chunk, :, :, None] * bbr
    kt = (jnp.einsum("ghp,tgpk->gtkh", c_re, wr, precision=hi)
          - jnp.einsum("ghp,tgpk->gtkh", c_im, wi, precision=hi))

    eye = jnp.eye(gpb, dtype=F32)
    tl = chunk * V7X_LANES
    st = 2 * gpb * n_state

    ktj = kt.reshape(nj, gpb, chunk, n_ch, n_ch).transpose(0, 2, 1, 3, 4)
    lag_blocks = (ktj[:, :, :, :, None, :] * eye[None, None, :, None, :, None]).astype(BF16)
    lag_blocks = lag_blocks.reshape(nj, chunk, V7X_LANES, V7X_LANES)

    def by_block(a):
        return a.reshape(a.shape[0], nj, gpb, n_state).transpose(1, 0, 2, 3)

    qr, qi = by_block(jnp.stack(prs[chunk - 1::-1])), by_block(jnp.stack(pis[chunk - 1::-1]))
    b4r = bbr.reshape(nj, 1, gpb, n_state, n_ch).transpose(0, 1, 2, 4, 3)
    b4i = bbi.reshape(nj, 1, gpb, n_state, n_ch).transpose(0, 1, 2, 4, 3)
    qr, qi = qr[:, :, :, None, :], qi[:, :, :, None, :]
    w2 = jnp.stack([qr * b4r - qi * b4i, qr * b4i + qi * b4r], axis=4)
    k_state = _spread_over_groups(w2.reshape(nj, tl, 2 * n_state), inner_r=n_ch, inner_c=n_state, gpb=gpb)

    er = pr[1:].reshape(chunk, nj, gpb, n_state).transpose(1, 2, 3, 0)[..., None]
    ei = pi[1:].reshape(chunk, nj, gpb, n_state).transpose(1, 2, 3, 0)[..., None]
    c4r = c_re.reshape(nj, gpb, n_ch, n_state).transpose(0, 1, 3, 2)[:, :, :, None, :]
    c4i = c_im.reshape(nj, gpb, n_ch, n_state).transpose(0, 1, 3, 2)[:, :, :, None, :]
    cl2 = jnp.stack([c4r * er - c4i * ei, -(c4r * ei + c4i * er)], axis=1)
    k_out = _spread_over_groups(cl2.reshape(nj, st, chunk * n_ch), inner_r=n_state, inner_c=n_ch, gpb=gpb)

    lam_t = jnp.stack([pr[chunk], pi[chunk]]).reshape(2, nj, gpb * n_state).transpose(1, 0, 2).reshape(nj, 1, st)
    return lag_blocks, k_state, k_out, lam_t


def _cross_body(x_ref, g_ref, wq_ref, kv_ref, qg_ref, kg_ref, wo_ref, ng_ref, o_ref, hn_ref, *, n_heads, hd):
    x = x_ref[...]
    h = _rms_rows(x, g_ref[...]).astype(BF16)
    q = jnp.dot(h, wq_ref[...], preferred_element_type=F32)
    mw = n_heads * hd
    scale = hd ** -0.5
    heads = []
    for hh in range(n_heads):
        cs = slice(hh * hd, (hh + 1) * hd)
        qn = (_rms_rows(q[:, cs], qg_ref[...]) * scale).astype(BF16)
        kn = _rms_rows(kv_ref[:, cs], kg_ref[...]).astype(BF16)
        v = kv_ref[:, mw + hh * hd:mw + (hh + 1) * hd].astype(BF16)
        s = lax.dot_general(qn, kn, (((1,), (1,)), ((), ())), preferred_element_type=F32)
        e = jnp.exp(s - jnp.max(s, axis=-1, keepdims=True))
        p = e / jnp.sum(e, axis=-1, keepdims=True)
        heads.append(jnp.dot(p.astype(BF16), v, preferred_element_type=F32))
    o = jnp.concatenate(heads, axis=-1).astype(BF16)
    y = x + jnp.dot(o, wo_ref[...], preferred_element_type=F32)
    o_ref[...] = y
    hn_ref[...] = _rms_rows(y, ng_ref[...]).astype(hn_ref.dtype)


def _cross_attention(x, g, w_cq, kv, cq_g, ck_g, w_co, next_g, *, bsz, seq, mem_len, tm=256):
    m, d = x.shape
    mw = w_cq.shape[1]
    hd = cq_g.shape[-1]
    n_heads = mw // hd
    tm = _tile(seq, tm)
    spb = seq // tm
    const = lambda a, b: pl.BlockSpec((a, b), lambda bb, i: (0, 0))
    rows = pl.BlockSpec((tm, d), lambda bb, i: (bb * spb + i, 0))
    return pl.pallas_call(
        functools.partial(_cross_body, n_heads=n_heads, hd=hd),
        grid=(bsz, spb),
        in_specs=[
            rows, const(1, d), const(d, mw),
            pl.BlockSpec((mem_len, 2 * mw), lambda bb, i: (bb, 0)),
            const(1, hd), const(1, hd), const(mw, d), const(1, d),
        ],
        out_specs=[rows, rows],
        out_shape=[jax.ShapeDtypeStruct((m, d), F32), jax.ShapeDtypeStruct((m, d), BF16)],
        compiler_params=_params("parallel", "parallel"),
        name="cross_attn",
    )(x, g.reshape(1, d), w_cq, kv, cq_g.reshape(1, hd), ck_g.reshape(1, hd), w_co, next_g.reshape(1, d))


def kernel(x, mem, norm_mix, w_in, q_norm, k_norm, conv_dw, conv_b, conv_ln_g, conv_ln_b, ssm_a_re, ssm_a_im, ssm_b_re, ssm_b_im, ssm_c_re, ssm_c_im, ssm_d, ssm_log_step, ssm_w_glu, ssm_b_glu, mix_out_norm, w_out, norm_cross, norm_mem, w_cq, w_ckv, cq_norm, ck_norm, w_co, norm_mlp, w_up, w_down):
    bsz, seq, d_model = x.shape
    mem_len = mem.shape[1]
    depth = w_in.shape[0]
    hd = q_norm.shape[-1]
    conv_w = conv_b.shape[-1]
    ssm_w = ssm_d.shape[-1]
    attn_w = w_out.shape[1] - conv_w - ssm_w
    n_heads = attn_w // hd
    conv_off = 3 * attn_w
    ssm_off = conv_off + 2 * conv_w
    assert ssm_off + ssm_w == w_in.shape[-1]
    assert conv_off % conv_w == 0 and ssm_off % ssm_w == 0 and ssm_w % V7X_LANES == 0

    xf = x.reshape(bsz * seq, d_model)
    memf = mem.reshape(bsz * mem_len, d_model)
    for l in range(depth):
        mix_g = mix_out_norm[l]

        proj = _panel_matmul([_rms_norm_bf16(xf, norm_mix[l])], w_in, l)

        attn = _dilated_attention(proj, q_norm[l], k_norm[l], bsz=bsz, seq=seq, n_heads=n_heads, hd=hd,
                                  k_col=n_heads, v_col=2 * n_heads)
        attn_n = _rms_norm_bf16(attn, mix_g[:attn_w])

        conv_n = _conformer_conv(proj, conv_dw[l].reshape(conv_dw.shape[1], conv_w), conv_b[l], conv_ln_g[l],
                                 conv_ln_b[l], mix_g[attn_w:attn_w + conv_w], bsz=bsz, seq=seq,
                                 a_col=conv_off // conv_w, g_col=conv_off // conv_w + 1)

        lag_blocks, k_state, k_out, lam_t = _ssm_chunk_matrices(
            ssm_a_re[l], ssm_a_im[l], ssm_b_re[l], ssm_b_im[l], ssm_c_re[l], ssm_c_im[l], ssm_log_step[l], SSM_CHUNK)
        y = _ssm_chunks(proj, lag_blocks, k_state, k_out, lam_t, bsz=bsz, seq=seq,
                        u_col=ssm_off // V7X_LANES, chunk=SSM_CHUNK)
        ssm_n = _ssm_post(y, proj, ssm_d[l], ssm_w_glu[l].astype(BF16), ssm_b_glu[l], mix_g[attn_w + conv_w:],
                          u_col=ssm_off // ssm_w)

        xf = _panel_matmul([attn_n, conv_n, ssm_n], w_out, l, res=xf)

        kv = _norm_matmul(memf, norm_mem[l], w_ckv[l].astype(BF16))
        xf, h_mlp = _cross_attention(xf, norm_cross[l], w_cq[l].astype(BF16), kv, cq_norm[l], ck_norm[l],
                                     w_co[l].astype(BF16), norm_mlp[l], bsz=bsz, seq=seq, mem_len=mem_len)

        hid = _panel_matmul([h_mlp], w_up, l, act="relu2", out_dtype=BF16)
        xf = _matmul_res(hid, w_down, l, xf)
    return xf.reshape(bsz, seq, d_model)
```

```python
import functools

import jax
import jax.numpy as jnp
from jax import lax
from jax.experimental import pallas as pl
from jax.experimental.pallas import tpu as pltpu

F32 = jnp.float32
BF16 = jnp.bfloat16

EPS = 1e-6
NEG_INF = -1e30
LOG2_E = 1.4426950408889634
DILATION_PATTERNS = ((128, 1), (512, 4), (2048, 16))

V7X_LANES = 128
V7X_SUBLANES = 8
V7X_VMEM_LIMIT_BYTES = 60000 * 1024
SSM_CHUNK = 8
SSM_RECURRENCE_UNROLL = 8


def _tile(n, pref, align=8):
    if n <= pref:
        return n
    for t in range(pref - pref % align, 0, -align):
        if n % t == 0:
            return t
    raise ValueError((n, pref, align))


def _params(*semantics):
    return pltpu.CompilerParams(dimension_semantics=semantics, vmem_limit_bytes=V7X_VMEM_LIMIT_BYTES)


def _rms_rows(x, g):
    return x * lax.rsqrt(jnp.mean(x * x, axis=-1, keepdims=True) + EPS) * g


def _norm_body(x_ref, g_ref, o_ref):
    o_ref[...] = _rms_rows(x_ref[...], g_ref[...]).astype(o_ref.dtype)


def _rms_norm_bf16(x, g, *, tm=256):
    m, k = x.shape
    tm = _tile(m, tm)
    return pl.pallas_call(
        _norm_body,
        grid=(m // tm,),
        in_specs=[pl.BlockSpec((tm, k), lambda i: (i, 0)), pl.BlockSpec((1, k), lambda i: (0, 0))],
        out_specs=pl.BlockSpec((tm, k), lambda i: (i, 0)),
        out_shape=jax.ShapeDtypeStruct((m, k), BF16),
        compiler_params=_params("parallel"),
        name="rms_norm",
    )(x, g.reshape(1, k))


WEIGHT_CAST_ROWS = 512


def _cast_panel(w_ref, wb_ref):
    rows = wb_ref.shape[0]
    rc = _tile(rows, WEIGHT_CAST_ROWS)

    def chunk(c, carry):
        sl = pl.ds(pl.multiple_of(c * rc, rc), rc)
        wb_ref[sl, :] = w_ref[sl, :].astype(wb_ref.dtype)
        return carry

    lax.fori_loop(0, rows // rc, chunk, 0)


def _panel_matmul_body(*refs, n_lhs, act, has_res):
    a_refs = refs[:n_lhs]
    w_ref = refs[n_lhs]
    r_ref = refs[n_lhs + 1] if has_res else None
    o_ref, wb_ref = refs[-2], refs[-1]

    @pl.when(pl.program_id(1) == 0)
    def _():
        _cast_panel(w_ref, wb_ref)

    acc = None
    k0 = 0
    for a_ref in a_refs:
        kk = a_ref.shape[1]
        part = jnp.dot(a_ref[...], wb_ref[k0:k0 + kk, :], preferred_element_type=F32)
        acc = part if acc is None else acc + part
        k0 += kk
    if act == "relu2":
        acc = jnp.square(jnp.maximum(acc, 0.0))
    if has_res:
        acc = r_ref[...] + acc
    o_ref[...] = acc.astype(o_ref.dtype)


def _panel_matmul(lhs, w, layer, *, res=None, act=None, out_dtype=F32, tm=1024, tn=512):
    m = lhs[0].shape[0]
    k, n = w.shape[1], w.shape[2]
    assert sum(a.shape[1] for a in lhs) == k
    tm, tn = _tile(m, tm), _tile(n, tn, V7X_LANES)
    in_specs = [pl.BlockSpec((tm, a.shape[1]), lambda j, i: (i, 0)) for a in lhs]
    in_specs.append(pl.BlockSpec((None, k, tn), lambda j, i: (layer, 0, j)))
    args = list(lhs) + [w]
    if res is not None:
        in_specs.append(pl.BlockSpec((tm, tn), lambda j, i: (i, j)))
        args.append(res)
    return pl.pallas_call(
        functools.partial(_panel_matmul_body, n_lhs=len(lhs), act=act, has_res=res is not None),
        grid=(n // tn, m // tm),
        in_specs=in_specs,
        out_specs=pl.BlockSpec((tm, tn), lambda j, i: (i, j)),
        out_shape=jax.ShapeDtypeStruct((m, n), out_dtype),
        scratch_shapes=[pltpu.VMEM((k, tn), BF16)],
        compiler_params=_params("parallel", "arbitrary"),
        name="panel_matmul",
    )(*args)


def _matmul_res_body(a_ref, w_ref, r_ref, o_ref, acc_ref):
    kk = pl.program_id(2)

    @pl.when(kk == 0)
    def _():
        acc_ref[...] = jnp.zeros_like(acc_ref)

    acc_ref[...] += jnp.dot(a_ref[...], w_ref[...].astype(BF16), preferred_element_type=F32)

    @pl.when(kk == pl.num_programs(2) - 1)
    def _():
        o_ref[...] = r_ref[...] + acc_ref[...]


def _matmul_res(a, w, layer, res, *, tm=1024, tn=1024, tk=2048):
    m, k = a.shape
    n = w.shape[2]
    tm, tn, tk = _tile(m, tm), _tile(n, tn, V7X_LANES), _tile(k, tk, V7X_LANES)
    return pl.pallas_call(
        _matmul_res_body,
        grid=(m // tm, n // tn, k // tk),
        in_specs=[
            pl.BlockSpec((tm, tk), lambda i, j, kk: (i, kk)),
            pl.BlockSpec((None, tk, tn), lambda i, j, kk: (layer, kk, j)),
            pl.BlockSpec((tm, tn), lambda i, j, kk: (i, j)),
        ],
        out_specs=pl.BlockSpec((tm, tn), lambda i, j, kk: (i, j)),
        out_shape=jax.ShapeDtypeStruct((m, n), F32),
        scratch_shapes=[pltpu.VMEM((tm, tn), F32)],
        compiler_params=_params("parallel", "parallel", "arbitrary"),
        name="matmul_res",
    )(a, w, res)


def _norm_matmul_body(x_ref, g_ref, w_ref, o_ref):
    h = _rms_rows(x_ref[...], g_ref[...]).astype(BF16)
    o_ref[...] = jnp.dot(h, w_ref[...], preferred_element_type=F32)


def _norm_matmul(x, g, w, *, tm=512):
    m, k = x.shape
    n = w.shape[1]
    tm = _tile(m, tm)
    return pl.pallas_call(
        _norm_matmul_body,
        grid=(m // tm,),
        in_specs=[
            pl.BlockSpec((tm, k), lambda i: (i, 0)),
            pl.BlockSpec((1, k), lambda i: (0, 0)),
            pl.BlockSpec((k, n), lambda i: (0, 0)),
        ],
        out_specs=pl.BlockSpec((tm, n), lambda i: (i, 0)),
        out_shape=jax.ShapeDtypeStruct((m, n), F32),
        compiler_params=_params("parallel"),
        name="norm_matmul",
    )(x, g.reshape(1, k), w)


ATTN_NORM_ROWS = 256
ATTN_BLOCKS_PER_ITER = 32


def _attn_body(q_ref, k_ref, v_ref, qg_ref, kg_ref, o_ref, qn_ref, kn_ref, vn_ref, acc_ref, m_ref, l_ref, bias_ref,
               *, dilations, base, blk, scale):
    seq, hd = q_ref.shape
    qg, kg = qg_ref[...], kg_ref[...]
    cls = seq // base
    rc = _tile(cls, ATTN_NORM_ROWS)
    qscale = scale * LOG2_E

    def norm_chunk(c, carry):
        for rb in range(base):
            src = pl.ds(c * (rc * base) + rb, rc, stride=base)
            dst = pl.ds(pl.multiple_of(rb * cls + c * rc, rc), rc)
            qn_ref[dst, :] = _rms_rows(q_ref[src, :], qg) * qscale
            kn_ref[dst, :] = _rms_rows(k_ref[src, :], kg)
            vn_ref[dst, :] = v_ref[src, :]
        return carry

    lax.fori_loop(0, cls // rc, norm_chunk, 0)

    qi = lax.broadcasted_iota(jnp.int32, (blk, 2 * blk), 0)
    kj = lax.broadcasted_iota(jnp.int32, (blk, 2 * blk), 1)
    qseg, kseg = blk // base, 2 * blk // base
    qt = (qi % qseg) * base + qi // qseg
    kt = (kj % kseg) * base + kj // kseg
    for e, (rel, off) in enumerate(((qi - kj, blk), (qi - kj, 0), (qt - kt, blk), (qt - kt, 0))):
        dist = rel + off
        bias_ref[e] = jnp.where((dist >= 0) & (dist <= blk), 0.0, NEG_INF).astype(F32)

    def load(ref, segs):
        parts = [ref[sl, :] for sl, _, _ in segs]
        return parts[0] if len(parts) == 1 else jnp.concatenate(parts, axis=0)

    def store(ref, segs, val):
        for sl, lo, hi in segs:
            ref[sl, :] = val[lo:hi]

    for pi, d in enumerate(dilations):
        nb = seq // (d * blk)
        first = pi == 0

        def views(idx, d=d, nb=nb):
            static = isinstance(idx, int)
            aligned = (lambda v, a: v) if static else pl.multiple_of
            r, n = idx // nb, idx % nb
            kb = max(n - 1, 0) if static else jnp.maximum(n - 1, 0)
            entry = 1 - (n - kb)
            if d % base == 0:
                e = d // base
                origin = (r % base) * cls + r // base
                if e == 1:
                    q_sl = pl.ds(aligned(origin + n * blk, blk), blk)
                    k_sl = pl.ds(aligned(origin + kb * blk, blk), 2 * blk)
                else:
                    q_sl = pl.ds(origin + n * (blk * e), blk, stride=e)
                    k_sl = pl.ds(origin + kb * (blk * e), 2 * blk, stride=e)
                return [(q_sl, 0, blk)], [(k_sl, 0, 2 * blk)], entry
            q_segs = [(pl.ds(aligned(rb * cls + n * qseg, qseg), qseg), rb * qseg, (rb + 1) * qseg)
                      for rb in range(base)]
            k_segs = [(pl.ds(aligned(rb * cls + kb * qseg, qseg), kseg), rb * kseg, (rb + 1) * kseg)
                      for rb in range(base)]
            return q_segs, k_segs, entry + 2

        def one_block(idx, views=views, first=first):
            q_segs, k_segs, entry = views(idx)
            q = load(qn_ref, q_segs).astype(BF16)
            kk = load(kn_ref, k_segs).astype(BF16)
            vv = load(vn_ref, k_segs).astype(BF16)
            s = lax.dot_general(q, kk, (((1,), (1,)), ((), ())), preferred_element_type=F32)
            s = s + bias_ref[entry]
            m_b = jnp.max(s, axis=-1, keepdims=True)
            p = jnp.exp2(s - m_b)
            l_b = jnp.sum(p, axis=-1, keepdims=True)
            o_b = jnp.dot(p.astype(BF16), vv, preferred_element_type=F32)
            if first:
                store(m_ref, q_segs, jnp.broadcast_to(m_b, (blk, hd)))
                store(l_ref, q_segs, jnp.broadcast_to(l_b, (blk, hd)))
                store(acc_ref, q_segs, o_b)
            else:
                m_o = load(m_ref, q_segs)
                m_n = jnp.maximum(m_o, m_b)
                alpha = jnp.exp2(m_o - m_n)
                beta = jnp.exp2(m_b - m_n)
                store(m_ref, q_segs, m_n)
                store(l_ref, q_segs, alpha * load(l_ref, q_segs) + beta * l_b)
                store(acc_ref, q_segs, alpha * load(acc_ref, q_segs) + beta * o_b)

        n_blocks = seq // blk
        per_iter = ATTN_BLOCKS_PER_ITER if n_blocks % ATTN_BLOCKS_PER_ITER == 0 else 1

        def block_group(g, carry, one_block=one_block, per_iter=per_iter):
            for u in range(per_iter):
                one_block(g * per_iter + u)
            return carry

        if n_blocks == per_iter:
            block_group(0, 0)
        else:
            lax.fori_loop(0, n_blocks // per_iter, block_group, 0)

    def out_chunk(c, carry):
        for rb in range(base):
            src = pl.ds(pl.multiple_of(rb * cls + c * rc, rc), rc)
            o_ref[pl.ds(c * (rc * base) + rb, rc, stride=base), :] = acc_ref[src, :] / l_ref[src, :]
        return carry

    lax.fori_loop(0, cls // rc, out_chunk, 0)


def _dilated_attention(proj, q_g, k_g, *, bsz, seq, n_heads, hd, k_col, v_col):
    assert hd == V7X_LANES
    blk = DILATION_PATTERNS[0][0] // DILATION_PATTERNS[0][1]
    for window, d in DILATION_PATTERNS:
        assert window == blk * d and seq % (d * blk) == 0 and seq // (d * blk) >= 2
    dilations = tuple(sorted((d for _, d in DILATION_PATTERNS), reverse=True))
    base = dilations[len(dilations) // 2]
    assert all(d == 1 or d % base == 0 for d in dilations) and blk % (base * V7X_SUBLANES) == 0
    head = lambda off: pl.BlockSpec((seq, hd), lambda b, h: (b, off + h))
    gain = pl.BlockSpec((1, hd), lambda b, h: (0, 0))
    return pl.pallas_call(
        functools.partial(_attn_body, dilations=dilations, base=base, blk=blk, scale=hd ** -0.5),
        grid=(bsz, n_heads),
        in_specs=[head(0), head(k_col), head(v_col), gain, gain],
        out_specs=head(0),
        out_shape=jax.ShapeDtypeStruct((bsz * seq, n_heads * hd), F32),
        scratch_shapes=[pltpu.VMEM((seq, hd), F32)] * 6 + [pltpu.VMEM((4, blk, 2 * blk), F32)],
        compiler_params=_params("parallel", "parallel"),
        name="dilated_attn",
    )(proj, proj, proj, q_g.reshape(1, hd), k_g.reshape(1, hd))


CONV_HALO_ROWS = 32
CONV_ROW_CHUNK = 64
CONV_COL_CHUNK = 256


def _conv_body(a_ref, g_ref, ah_ref, gh_ref, dw_ref, db_ref, lg_ref, lb_ref, mg_ref, o_ref, hs, cbuf, *, width):
    ts, cw = a_ref.shape
    halo = ah_ref.shape[0]
    sub = hs.shape[0]
    i = pl.program_id(1)
    hs[0, halo:halo + ts, :] = a_ref[...] * jax.nn.sigmoid(g_ref[...])

    @pl.when(i > 0)
    def _():
        hs[0, 0:halo, :] = ah_ref[...] * jax.nn.sigmoid(gh_ref[...])

    @pl.when(i == 0)
    def _():
        hs[0, 0:halo, :] = jnp.zeros((halo, cw), F32)

    off = halo - (width - 1)
    rows_b = halo + ts - sub
    for b in range(1, sub):
        hs[b, 0:rows_b, :] = hs[0, b:b + rows_b, :]

    rc, cc = min(CONV_ROW_CHUNK, ts), min(CONV_COL_CHUNK, cw)
    for r0 in range(0, ts, rc):
        for c0 in range(0, cw, cc):
            acc = jnp.zeros((rc, cc), F32)
            for j in range(width):
                a8, b = (off + j) // sub * sub, (off + j) % sub
                acc += hs[b, r0 + a8:r0 + a8 + rc, c0:c0 + cc] * dw_ref[j:j + 1, c0:c0 + cc]
            cbuf[r0:r0 + rc, c0:c0 + cc] = acc + db_ref[:, c0:c0 + cc]

    for r0 in range(0, ts, rc):
        y = cbuf[r0:r0 + rc, :]
        yc = y - jnp.mean(y, axis=-1, keepdims=True)
        yn = yc * lax.rsqrt(jnp.mean(yc * yc, axis=-1, keepdims=True) + EPS) * lg_ref[...] + lb_ref[...]
        z = jax.nn.silu(yn)
        o_ref[r0:r0 + rc, :] = _rms_rows(z, mg_ref[...]).astype(o_ref.dtype)


def _conformer_conv(proj, dw, db, ln_g, ln_b, mix_g, *, bsz, seq, a_col, g_col, ts=256):
    n_cols = proj.shape[1]
    width, cw = dw.shape
    assert width - 1 <= CONV_HALO_ROWS
    ts = _tile(seq, ts)
    assert ts % CONV_HALO_ROWS == 0
    hpt = ts // CONV_HALO_ROWS
    view = proj.reshape(bsz, seq, n_cols)

    def main(col):
        return pl.BlockSpec((None, ts, cw), lambda b, i: (b, i, col))

    def halo(col):
        return pl.BlockSpec((None, CONV_HALO_ROWS, cw), lambda b, i: (b, jnp.maximum(i * hpt - 1, 0), col))

    vec = pl.BlockSpec((1, cw), lambda b, i: (0, 0))
    out = pl.pallas_call(
        functools.partial(_conv_body, width=width),
        grid=(bsz, seq // ts),
        in_specs=[main(a_col), main(g_col), halo(a_col), halo(g_col),
                  pl.BlockSpec((width, cw), lambda b, i: (0, 0)), vec, vec, vec, vec],
        out_specs=pl.BlockSpec((None, ts, cw), lambda b, i: (b, i, 0)),
        out_shape=jax.ShapeDtypeStruct((bsz, seq, cw), BF16),
        scratch_shapes=[pltpu.VMEM((V7X_SUBLANES, CONV_HALO_ROWS + ts, cw), F32), pltpu.VMEM((ts, cw), F32)],
        compiler_params=_params("parallel", "parallel"),
        name="conformer_conv",
    )(view, view, view, view, dw, db.reshape(1, cw), ln_g.reshape(1, cw), ln_b.reshape(1, cw), mix_g.reshape(1, cw))
    return out.reshape(bsz * seq, cw)


def _ssm_chunk_body(u_ref, d_ref, ks_ref, ko_ref, lt_ref, y_ref, kin_ref, ul_ref, inj_ref, x0_ref, *, chunk):
    seq, lanes = u_ref.shape
    nc = seq // chunk
    half = lt_ref.shape[1] // 2

    @pl.when(pl.program_id(1) == 0)
    def _():
        zero = jnp.zeros((lanes, lanes), kin_ref.dtype)
        for s in range(chunk):
            for t in range(chunk):
                kin_ref[s * lanes:(s + 1) * lanes, t * lanes:(t + 1) * lanes] = d_ref[t - s] if t >= s else zero

    for s in range(chunk):
        ul_ref[:, s * lanes:(s + 1) * lanes] = u_ref[pl.ds(s, nc, stride=chunk), :].astype(ul_ref.dtype)
    u = ul_ref[...]
    inj_ref[...] = jnp.dot(u, ks_ref[...], preferred_element_type=F32)
    lr, li = lt_ref[:, :half], lt_ref[:, half:]

    def step(c, carry):
        xr, xi = carry
        x0_ref[pl.ds(c, 1), :half] = xr
        x0_ref[pl.ds(c, 1), half:] = xi
        row = inj_ref[pl.ds(c, 1), :]
        return lr * xr - li * xi + row[:, :half], lr * xi + li * xr + row[:, half:]

    zero = jnp.zeros((1, half), F32)
    lax.fori_loop(0, nc, step, (zero, zero), unroll=SSM_RECURRENCE_UNROLL)
    y = jnp.dot(u, kin_ref[...], preferred_element_type=F32)
    y += jnp.dot(x0_ref[...].astype(BF16), ko_ref[...], preferred_element_type=F32)
    for t in range(chunk):
        y_ref[pl.ds(t, nc, stride=chunk), :] = y[:, t * lanes:(t + 1) * lanes]


def _ssm_chunks(proj, lag_blocks, k_state, k_out, lam_t, *, bsz, seq, u_col, chunk):
    nj, _, lanes, _ = lag_blocks.shape
    tl = chunk * lanes
    st = k_state.shape[2]
    nc = seq // chunk
    mat = lambda a, b: pl.BlockSpec((None, a, b), lambda j, bb: (j, 0, 0))
    return pl.pallas_call(
        functools.partial(_ssm_chunk_body, chunk=chunk),
        grid=(nj, bsz),
        in_specs=[
            pl.BlockSpec((seq, lanes), lambda j, bb: (bb, u_col + j)),
            pl.BlockSpec((None, chunk, lanes, lanes), lambda j, bb: (j, 0, 0, 0)),
            mat(tl, st), mat(st, tl), mat(1, st),
        ],
        out_specs=pl.BlockSpec((seq, lanes), lambda j, bb: (bb, j)),
        out_shape=jax.ShapeDtypeStruct((bsz * seq, nj * lanes), F32),
        scratch_shapes=[pltpu.VMEM((tl, tl), BF16), pltpu.VMEM((nc, tl), BF16),
                        pltpu.VMEM((nc, st), F32), pltpu.VMEM((nc, st), F32)],
        compiler_params=_params("parallel", "arbitrary"),
        name="ssm_chunks",
    )(proj, lag_blocks, k_state, k_out, lam_t)


def _ssm_post_body(y_ref, u_ref, d_ref, wg_ref, bg_ref, mg_ref, o_ref):
    z = jax.nn.gelu(y_ref[...] + d_ref[...] * u_ref[...])
    gate = jnp.dot(z.astype(BF16), wg_ref[...], preferred_element_type=F32) + bg_ref[...]
    out = z * jax.nn.sigmoid(gate)
    o_ref[...] = _rms_rows(out, mg_ref[...]).astype(o_ref.dtype)


def _ssm_post(y, proj, d_skip, w_glu, b_glu, mix_g, *, u_col, tm=512):
    m, sw = y.shape
    tm = _tile(m, tm)
    vec = pl.BlockSpec((1, sw), lambda i: (0, 0))
    return pl.pallas_call(
        _ssm_post_body,
        grid=(m // tm,),
        in_specs=[pl.BlockSpec((tm, sw), lambda i: (i, 0)), pl.BlockSpec((tm, sw), lambda i: (i, u_col)),
                  vec, pl.BlockSpec((sw, sw), lambda i: (0, 0)), vec, vec],
        out_specs=pl.BlockSpec((tm, sw), lambda i: (i, 0)),
        out_shape=jax.ShapeDtypeStruct((m, sw), BF16),
        compiler_params=_params("parallel"),
        name="ssm_post",
    )(y, proj, d_skip.reshape(1, sw), w_glu, b_glu.reshape(1, sw), mix_g.reshape(1, sw))


def _spread_over_groups(compact, *, inner_r, inner_c, gpb):
    _, n_rows, n_c = compact.shape
    n_cols = n_c * gpb
    col = jnp.arange(n_cols)
    src = (col // (gpb * inner_c)) * inner_c + col % inner_c
    spread = (jnp.arange(n_c)[:, None] == src[None, :]).astype(BF16)
    wide = jnp.einsum("jrc,cn->jrn", compact.astype(BF16), spread, preferred_element_type=BF16)
    same = ((jnp.arange(n_rows) // inner_r) % gpb)[:, None] == ((col // inner_c) % gpb)[None, :]
    return jnp.where(same[None], wide, jnp.zeros((), BF16))


def _ssm_chunk_matrices(a_re, a_im, b_re, b_im, c_re, c_im, log_step, chunk):
    n_groups, n_state = a_re.shape
    n_ch = b_re.shape[-1]
    gpb = V7X_LANES // n_ch
    assert V7X_LANES % n_ch == 0 and n_groups % gpb == 0
    nj = n_groups // gpb
    hi = lax.Precision.HIGHEST

    step = jnp.exp(log_step)[:, None]
    mag, ang = jnp.exp(a_re * step), a_im * step
    lr, li = mag * jnp.cos(ang), mag * jnp.sin(ang)
    den = a_re * a_re + a_im * a_im
    fr = ((lr - 1.0) * a_re + li * a_im) / den
    fi = (li * a_re - (lr - 1.0) * a_im) / den
    bbr = fr[..., None] * b_re - fi[..., None] * b_im
    bbi = fr[..., None] * b_im + fi[..., None] * b_re

    prs, pis = [jnp.ones_like(lr)], [jnp.zeros_like(lr)]
    for _ in range(chunk):
        prs.append(prs[-1] * lr - pis[-1] * li)
        pis.append(prs[-2] * li + pis[-1] * lr)
    pr, pi = jnp.stack(prs), jnp.stack(pis)

    wr = pr[:chunk, :, :, None] * bbr - pi[:chunk, :, :, None] * bbi
    wi = pr[:chunk, :, :, None] * bbi + pi[:chunk, :, :, None] * bbr
    kt = (jnp.einsum("ghp,tgpk->gtkh", c_re, wr, precision=hi)
          - jnp.einsum("ghp,tgpk->gtkh", c_im, wi, precision=hi))

    eye = jnp.eye(gpb, dtype=F32)
    tl = chunk * V7X_LANES
    st = 2 * gpb * n_state

    ktj = kt.reshape(nj, gpb, chunk, n_ch, n_ch).transpose(0, 2, 1, 3, 4)
    lag_blocks = (ktj[:, :, :, :, None, :] * eye[None, None, :, None, :, None]).astype(BF16)
    lag_blocks = lag_blocks.reshape(nj, chunk, V7X_LANES, V7X_LANES)

    def by_block(a):
        return a.reshape(a.shape[0], nj, gpb, n_state).transpose(1, 0, 2, 3)

    qr, qi = by_block(jnp.stack(prs[chunk - 1::-1])), by_block(jnp.stack(pis[chunk - 1::-1]))
    b4r = bbr.reshape(nj, 1, gpb, n_state, n_ch).transpose(0, 1, 2, 4, 3)
    b4i = bbi.reshape(nj, 1, gpb, n_state, n_ch).transpose(0, 1, 2, 4, 3)
    qr, qi = qr[:, :, :, None, :], qi[:, :, :, None, :]
    w2 = jnp.stack([qr * b4r - qi * b4i, qr * b4i + qi * b4r], axis=4)
    k_state = _spread_over_groups(w2.reshape(nj, tl, 2 * n_state), inner_r=n_ch, inner_c=n_state, gpb=gpb)

    er = pr[1:].reshape(chunk, nj, gpb, n_state).transpose(1, 2, 3, 0)[..., None]
    ei = pi[1:].reshape(chunk, nj, gpb, n_state).transpose(1, 2, 3, 0)[..., None]
    c4r = c_re.reshape(nj, gpb, n_ch, n_state).transpose(0, 1, 3, 2)[:, :, :, None, :]
    c4i = c_im.reshape(nj, gpb, n_ch, n_state).transpose(0, 1, 3, 2)[:, :, :, None, :]
    cl2 = jnp.stack([c4r * er - c4i * ei, -(c4r * ei + c4i * er)], axis=1)
    k_out = _spread_over_groups(cl2.reshape(nj, st, chunk * n_ch), inner_r=n_state, inner_c=n_ch, gpb=gpb)

    lam_t = jnp.stack([pr[chunk], pi[chunk]]).reshape(2, nj, gpb * n_state).transpose(1, 0, 2).reshape(nj, 1, st)
    return lag_blocks, k_state, k_out, lam_t


def _cross_body(x_ref, g_ref, wq_ref, kv_ref, qg_ref, kg_ref, wo_ref, ng_ref, o_ref, hn_ref, *, n_heads, hd):
    x = x_ref[...]
    h = _rms_rows(x, g_ref[...]).astype(BF16)
    q = jnp.dot(h, wq_ref[...], preferred_element_type=F32)
    mw = n_heads * hd
    scale = hd ** -0.5
    heads = []
    for hh in range(n_heads):
        cs = slice(hh * hd, (hh + 1) * hd)
        qn = (_rms_rows(q[:, cs], qg_ref[...]) * scale).astype(BF16)
        kn = _rms_rows(kv_ref[:, cs], kg_ref[...]).astype(BF16)
        v = kv_ref[:, mw + hh * hd:mw + (hh + 1) * hd].astype(BF16)
        s = lax.dot_general(qn, kn, (((1,), (1,)), ((), ())), preferred_element_type=F32)
        e = jnp.exp(s - jnp.max(s, axis=-1, keepdims=True))
        p = e / jnp.sum(e, axis=-1, keepdims=True)
        heads.append(jnp.dot(p.astype(BF16), v, preferred_element_type=F32))
    o = jnp.concatenate(heads, axis=-1).astype(BF16)
    y = x + jnp.dot(o, wo_ref[...], preferred_element_type=F32)
    o_ref[...] = y
    hn_ref[...] = _rms_rows(y, ng_ref[...]).astype(hn_ref.dtype)


def _cross_attention(x, g, w_cq, kv, cq_g, ck_g, w_co, next_g, *, bsz, seq, mem_len, tm=512):
    m, d = x.shape
    mw = w_cq.shape[1]
    hd = cq_g.shape[-1]
    n_heads = mw // hd
    tm = _tile(seq, tm)
    spb = seq // tm
    const = lambda a, b: pl.BlockSpec((a, b), lambda bb, i: (0, 0))
    weight = lambda a, b: pl.BlockSpec((a, b), lambda bb, i: (0, 0), pipeline_mode=pl.Buffered(1))
    rows = pl.BlockSpec((tm, d), lambda bb, i: (bb * spb + i, 0))
    return pl.pallas_call(
        functools.partial(_cross_body, n_heads=n_heads, hd=hd),
        grid=(bsz, spb),
        in_specs=[
            rows, const(1, d), weight(d, mw),
            pl.BlockSpec((mem_len, 2 * mw), lambda bb, i: (bb, 0)),
            const(1, hd), const(1, hd), weight(mw, d), const(1, d),
        ],
        out_specs=[rows, rows],
        out_shape=[jax.ShapeDtypeStruct((m, d), F32), jax.ShapeDtypeStruct((m, d), BF16)],
        compiler_params=_params("parallel", "parallel"),
        name="cross_attn",
    )(x, g.reshape(1, d), w_cq, kv, cq_g.reshape(1, hd), ck_g.reshape(1, hd), w_co, next_g.reshape(1, d))


def kernel(x, mem, norm_mix, w_in, q_norm, k_norm, conv_dw, conv_b, conv_ln_g, conv_ln_b, ssm_a_re, ssm_a_im, ssm_b_re, ssm_b_im, ssm_c_re, ssm_c_im, ssm_d, ssm_log_step, ssm_w_glu, ssm_b_glu, mix_out_norm, w_out, norm_cross, norm_mem, w_cq, w_ckv, cq_norm, ck_norm, w_co, norm_mlp, w_up, w_down):
    bsz, seq, d_model = x.shape
    mem_len = mem.shape[1]
    depth = w_in.shape[0]
    hd = q_norm.shape[-1]
    conv_w = conv_b.shape[-1]
    ssm_w = ssm_d.shape[-1]
    attn_w = w_out.shape[1] - conv_w - ssm_w
    n_heads = attn_w // hd
    conv_off = 3 * attn_w
    ssm_off = conv_off + 2 * conv_w
    assert ssm_off + ssm_w == w_in.shape[-1]
    assert conv_off % conv_w == 0 and ssm_off % ssm_w == 0 and ssm_w % V7X_LANES == 0

    xf = x.reshape(bsz * seq, d_model)
    memf = mem.reshape(bsz * mem_len, d_model)
    for l in range(depth):
        mix_g = mix_out_norm[l]

        proj = _panel_matmul([_rms_norm_bf16(xf, norm_mix[l])], w_in, l)

        attn = _dilated_attention(proj, q_norm[l], k_norm[l], bsz=bsz, seq=seq, n_heads=n_heads, hd=hd,
                                  k_col=n_heads, v_col=2 * n_heads)
        attn_n = _rms_norm_bf16(attn, mix_g[:attn_w])

        conv_n = _conformer_conv(proj, conv_dw[l].reshape(conv_dw.shape[1], conv_w), conv_b[l], conv_ln_g[l],
                                 conv_ln_b[l], mix_g[attn_w:attn_w + conv_w], bsz=bsz, seq=seq,
                                 a_col=conv_off // conv_w, g_col=conv_off // conv_w + 1)

        lag_blocks, k_state, k_out, lam_t = _ssm_chunk_matrices(
            ssm_a_re[l], ssm_a_im[l], ssm_b_re[l], ssm_b_im[l], ssm_c_re[l], ssm_c_im[l], ssm_log_step[l], SSM_CHUNK)
        y = _ssm_chunks(proj, lag_blocks, k_state, k_out, lam_t, bsz=bsz, seq=seq,
                        u_col=ssm_off // V7X_LANES, chunk=SSM_CHUNK)
        ssm_n = _ssm_post(y, proj, ssm_d[l], ssm_w_glu[l].astype(BF16), ssm_b_glu[l], mix_g[attn_w + conv_w:],
                          u_col=ssm_off // ssm_w)

        xf = _panel_matmul([attn_n, conv_n, ssm_n], w_out, l, res=xf)

        kv = _norm_matmul(memf, norm_mem[l], w_ckv[l].astype(BF16))
        xf, h_mlp = _cross_attention(xf, norm_cross[l], w_cq[l].astype(BF16), kv, cq_norm[l], ck_norm[l],
                                     w_co[l].astype(BF16), norm_mlp[l], bsz=bsz, seq=seq, mem_len=mem_len)

        hid = _panel_matmul([h_mlp], w_up, l, act="relu2", out_dtype=BF16)
        xf = _matmul_res(hid, w_down, l, xf)
    return xf.reshape(bsz, seq, d_model)
```
